```python
import math
import jax, jax.numpy as jnp
from jax import lax
import numpy as np

D_MODEL = 1024
BATCH = 8
SEQ = 2048
DEPTH = 4

CHUNK = 64
N_MEM = 256
Q_BLOCK = 128
NORM_EPS = 1e-6
NEG_INF = -1e30

A_HEADS = 4
A_DK = 128
A_DV = 128
A_QK = A_HEADS * A_DK
A_V = A_HEADS * A_DV
SHORT_CONV = 4
B_WIDTH = 512
B_BLOCKS = 8
B_BLOCK_DIM = B_WIDTH // B_BLOCKS
B_CONV = 4
LRU_C = 8.0
C_HEADS = 4
C_DK = 64
C_DV = 2 * C_DK
D_HEADS = 4
Q_LORA = 256
KV_LORA = 128
D_NOPE = 64
D_ROPE = 32
D_DV = 128
ROPE_THETA = 10000.0
X_HEADS = 4
X_DH = 128
FF_HIDDEN = -(-(8 * D_MODEL) // (3 * 256)) * 256

EV_COLS = 2 * A_QK + 2 * A_V + 2 * A_HEADS + 2 * B_WIDTH
EV_OUT = A_V + B_WIDTH
OD_COLS = 4 * C_HEADS * C_DK + C_HEADS * C_DV + Q_LORA + KV_LORA + D_ROPE
OD_OUT = C_HEADS * C_DV + D_HEADS * D_DV
N_EVEN = (DEPTH + 1) // 2
N_ODD = DEPTH // 2

kernel_name = 'hybrid_deltanet_rglru_diffattn_mla_trunk'


def _split(x, sizes):
    return jnp.split(x, np.cumsum(sizes)[:-1].tolist(), axis=-1)


def _rmsnorm(x, g):
    xf = x.astype(jnp.float32)
    y = xf * lax.rsqrt(jnp.mean(xf * xf, axis=-1, keepdims=True) + NORM_EPS)
    return (y * g.astype(jnp.float32)).astype(x.dtype)


def _l2norm(x):
    xf = x.astype(jnp.float32)
    return xf * lax.rsqrt(jnp.sum(xf * xf, axis=-1, keepdims=True) + NORM_EPS)


def _causal_dwconv(x, w):
    k, s = w.shape[0], x.shape[1]
    xp = jnp.pad(x, ((0, 0), (k - 1, 0), (0, 0)))
    y = xp[:, 0:s] * w[0]
    for j in range(1, k):
        y = y + xp[:, j:j + s] * w[j]
    return y


def _unit_lower_inverse(m):
    eye = jnp.eye(m.shape[-1], dtype=m.dtype)
    n = -m
    acc = eye + n
    p = n
    for _ in range(CHUNK.bit_length() - 2):
        p = p @ p
        acc = acc @ (eye + p)
    return acc


def _gated_delta_rule(q, k, v, g, beta):
    f32 = jnp.float32
    b, s, h, dk = q.shape
    dv = v.shape[-1]
    nc = s // CHUNK

    def chunks(t):
        t = t.astype(f32).reshape((b, nc, CHUNK, h) + t.shape[3:])
        return jnp.moveaxis(t, 3, 1)

    q = chunks(q) * (dk ** -0.5)
    k = chunks(k)
    v = chunks(v)
    g = chunks(g)
    beta = chunks(beta)
    cum = jnp.cumsum(g, axis=-1)
    incl = jnp.tril(jnp.ones((CHUNK, CHUNK), dtype=bool))
    strict = jnp.tril(jnp.ones((CHUNK, CHUNK), dtype=bool), -1)
    rel = cum[..., :, None] - cum[..., None, :]
    decay = jnp.where(incl, jnp.exp(jnp.where(incl, rel, 0.0)), 0.0)
    kb = k * beta[..., None]
    m = jnp.where(strict, jnp.einsum('bhnid,bhnjd->bhnij', kb, k) * decay, 0.0)
    t_inv = _unit_lower_inverse(m)
    w = jnp.einsum('bhnij,bhnjd->bhnid', t_inv, kb * jnp.exp(cum)[..., None])
    u = jnp.einsum('bhnij,bhnje->bhnie', t_inv, v * beta[..., None])
    qk = jnp.where(incl, jnp.einsum('bhnid,bhnjd->bhnij', q, k) * decay, 0.0)
    q_dec = q * jnp.exp(cum)[..., None]
    k_dec = k * jnp.exp(cum[..., -1:] - cum)[..., None]
    last = jnp.exp(cum[..., -1])

    def step(state, xs):
        w_c, u_c, q_c, qk_c, k_c, last_c = xs
        v_new = u_c - jnp.einsum('bhcd,bhde->bhce', w_c, state)
        o_c = (jnp.einsum('bhcd,bhde->bhce', q_c, state)
               + jnp.einsum('bhij,bhje->bhie', qk_c, v_new))
        state = state * last_c[..., None, None] + jnp.einsum('bhcd,bhce->bhde', k_c, v_new)
        return state, o_c

    xs = tuple(jnp.moveaxis(t, 2, 0) for t in (w, u, q_dec, qk, k_dec, last))
    _, o = lax.scan(step, jnp.zeros((b, h, dk, dv), f32), xs)
    return jnp.transpose(o, (1, 0, 3, 2, 4)).reshape(b, s, h, dv)


def _linear_recurrence_combine(left, right):
    a_l, b_l = left
    a_r, b_r = right
    return a_l * a_r, a_r * b_l + b_r


def _alibi_slopes(n_heads):
    return 2.0 ** (-8.0 * jnp.arange(1, n_heads + 1, dtype=jnp.float32) / n_heads)


def _chunk_causal_mask(start, seq):
    t = start + jnp.arange(Q_BLOCK, dtype=jnp.int32)
    s = jnp.arange(seq, dtype=jnp.int32)
    return (s[None, :] // CHUNK) <= (t[:, None] // CHUNK)


def _sweep_query_blocks(block_fn, seq):
    starts = jnp.arange(seq // Q_BLOCK, dtype=jnp.int32) * Q_BLOCK
    out = lax.map(block_fn, starts)
    nqb, b, h, qb, dv = out.shape
    return jnp.transpose(out, (1, 0, 3, 2, 4)).reshape(b, nqb * qb, h * dv)


def _diff_attention(q, k, v, lam, positions):
    b, h2, s, dk = q.shape
    slopes = jnp.repeat(_alibi_slopes(C_HEADS), 2)
    scale = dk ** -0.5

    def block(start):
        qb = lax.dynamic_slice_in_dim(q, start, Q_BLOCK, axis=2)
        pq = lax.dynamic_slice_in_dim(positions, start, Q_BLOCK, axis=1)
        dist = jnp.abs(pq[:, :, None] - positions[:, None, :]).astype(jnp.float32)
        sc = jnp.einsum('bhqd,bhkd->bhqk', qb, k).astype(jnp.float32) * scale
        sc = sc - slopes[None, :, None, None] * dist[:, None]
        sc = jnp.where(_chunk_causal_mask(start, s), sc, NEG_INF)
        p = jax.nn.softmax(sc, axis=-1).reshape(b, C_HEADS, 2, Q_BLOCK, s)
        p = (p[:, :, 0] - lam * p[:, :, 1]).astype(v.dtype)
        return jnp.einsum('bhqk,bhkd->bhqd', p, v)

    return _sweep_query_blocks(block, s)


def _mla_attention(q, k, v):
    s, dk = q.shape[2], q.shape[3]
    scale = dk ** -0.5

    def block(start):
        qb = lax.dynamic_slice_in_dim(q, start, Q_BLOCK, axis=2)
        sc = jnp.einsum('bhqd,bhkd->bhqk', qb, k).astype(jnp.float32) * scale
        sc = jnp.where(_chunk_causal_mask(start, s), sc, NEG_INF)
        p = jax.nn.softmax(sc, axis=-1).astype(v.dtype)
        return jnp.einsum('bhqk,bhkd->bhqd', p, v)

    return _sweep_query_blocks(block, s)


def _rope(x, positions):
    half = x.shape[-1] // 2
    inv_freq = ROPE_THETA ** (-jnp.arange(half, dtype=jnp.float32) / half)
    ang = positions.astype(jnp.float32)[:, :, None] * inv_freq
    cos = jnp.cos(ang)[:, :, None, :]
    sin = jnp.sin(ang)[:, :, None, :]
    x1 = x[..., :half].astype(jnp.float32)
    x2 = x[..., half:].astype(jnp.float32)
    return jnp.concatenate([x1 * cos - x2 * sin, x2 * cos + x1 * sin], axis=-1).astype(x.dtype)


def _even_mixer(h, w_in, conv_qkv, a_log, dt_bias, o_norm, conv_b_w, conv_b_b,
                gate_a_w, gate_a_b, gate_x_w, gate_x_b, lru_l, w_out):
    f32 = jnp.float32
    b, s, _ = h.shape
    qkv, z, beta_raw, decay_raw, xb, gb = _split(
        h @ w_in, [2 * A_QK + A_V, A_V, A_HEADS, A_HEADS, B_WIDTH, B_WIDTH])
    qkv = jax.nn.silu(_causal_dwconv(qkv, conv_qkv))
    q, k, v = _split(qkv, [A_QK, A_QK, A_V])
    q = _l2norm(q.reshape(b, s, A_HEADS, A_DK))
    k = _l2norm(k.reshape(b, s, A_HEADS, A_DK))
    v = v.reshape(b, s, A_HEADS, A_DV)
    beta = jax.nn.sigmoid(beta_raw.astype(f32))
    g = -jnp.exp(a_log.astype(f32)) * jax.nn.softplus(decay_raw.astype(f32) + dt_bias.astype(f32))
    o = _gated_delta_rule(q, k, v, g, beta)
    o = _rmsnorm(o, o_norm) * jax.nn.silu(z.reshape(b, s, A_HEADS, A_DV).astype(f32))
    y_a = o.reshape(b, s, A_V).astype(h.dtype)
    xb = _causal_dwconv(xb, conv_b_w) + conv_b_b
    xblk = xb.reshape(b, s, B_BLOCKS, B_BLOCK_DIM)
    r = jax.nn.sigmoid((jnp.einsum('bsnd,nde->bsne', xblk, gate_a_w).reshape(b, s, B_WIDTH)
                        + gate_a_b).astype(f32))
    i = jax.nn.sigmoid((jnp.einsum('bsnd,nde->bsne', xblk, gate_x_w).reshape(b, s, B_WIDTH)
                        + gate_x_b).astype(f32))
    log_a = -LRU_C * r * jax.nn.softplus(-lru_l.astype(f32))
    a = jnp.exp(log_a)
    u = jnp.sqrt(-jnp.expm1(2.0 * log_a)) * (i * xb.astype(f32))
    _, hs = lax.associative_scan(_linear_recurrence_combine, (a, u), axis=1)
    y_b = (jax.nn.gelu(gb.astype(f32)) * hs).astype(h.dtype)
    return jnp.concatenate([y_a, y_b], axis=-1) @ w_out


def _odd_mixer(h, positions, layer_idx, w_in, c_q_norm, c_k_norm, lam_q1, lam_k1, lam_q2, lam_k2,
               c_sub_norm, q_lat_norm, w_uq, kv_lat_norm, w_ukv, d_q_norm, d_k_norm, w_out):
    f32 = jnp.float32
    b, s, _ = h.shape
    qc, kc, vc, q_lat, kv_lat, k_rope = _split(
        h @ w_in, [2 * C_HEADS * C_DK, 2 * C_HEADS * C_DK, C_HEADS * C_DV, Q_LORA, KV_LORA, D_ROPE])
    qc = jnp.transpose(_rmsnorm(qc.reshape(b, s, 2 * C_HEADS, C_DK), c_q_norm), (0, 2, 1, 3))
    kc = jnp.transpose(_rmsnorm(kc.reshape(b, s, 2 * C_HEADS, C_DK), c_k_norm), (0, 2, 1, 3))
    vc = jnp.transpose(vc.reshape(b, s, C_HEADS, C_DV), (0, 2, 1, 3))
    lam_init = 0.8 - 0.6 * math.exp(-0.3 * layer_idx)
    lam = (jnp.exp(jnp.sum(lam_q1.astype(f32) * lam_k1.astype(f32)))
           - jnp.exp(jnp.sum(lam_q2.astype(f32) * lam_k2.astype(f32))) + lam_init)
    yc = _diff_attention(qc, kc, vc, lam, positions)
    yc = (_rmsnorm(yc.reshape(b, s, C_HEADS, C_DV), c_sub_norm) * (1.0 - lam_init)).reshape(b, s, C_HEADS * C_DV)
    qd = (_rmsnorm(q_lat, q_lat_norm) @ w_uq).reshape(b, s, D_HEADS, D_NOPE + D_ROPE)
    kvd = (_rmsnorm(kv_lat, kv_lat_norm) @ w_ukv).reshape(b, s, D_HEADS, D_NOPE + D_DV)
    k_nope, vd = _split(kvd, [D_NOPE, D_DV])
    kd = jnp.concatenate([k_nope, jnp.broadcast_to(k_rope[:, :, None, :], (b, s, D_HEADS, D_ROPE))], axis=-1)
    qd = _rmsnorm(qd, d_q_norm)
    kd = _rmsnorm(kd, d_k_norm)
    qd = jnp.concatenate([qd[..., :D_NOPE], _rope(qd[..., D_NOPE:], positions)], axis=-1)
    kd = jnp.concatenate([kd[..., :D_NOPE], _rope(kd[..., D_NOPE:], positions)], axis=-1)
    yd = _mla_attention(jnp.transpose(qd, (0, 2, 1, 3)), jnp.transpose(kd, (0, 2, 1, 3)),
                        jnp.transpose(vd, (0, 2, 1, 3)))
    return jnp.concatenate([yc, yd], axis=-1) @ w_out


def _memory_cross_attention(h, m, wq, wkv, q_norm, k_norm, wo):
    b, s, _ = h.shape
    nm = m.shape[1]
    q = _rmsnorm((h @ wq).reshape(b, s, X_HEADS, X_DH), q_norm)
    kv = (m @ wkv).reshape(b, nm, 2, X_HEADS, X_DH)
    k = _rmsnorm(kv[:, :, 0], k_norm)
    v = kv[:, :, 1]
    sc = jnp.einsum('bshd,bmhd->bhsm', q, k).astype(jnp.float32) * (X_DH ** -0.5)
    p = jax.nn.softmax(sc, axis=-1).astype(v.dtype)
    o = jnp.einsum('bhsm,bmhd->bshd', p, v).reshape(b, s, X_HEADS * X_DH)
    return o @ wo


def _swiglu(h, w_in, w_out):
    gate, up = jnp.split(h @ w_in, 2, axis=-1)
    return (jax.nn.silu(gate) * up) @ w_out


def setup_inputs(seed: int = 0) -> dict:
    key = jax.random.key(seed)
    ks = iter(jax.random.split(key, 64))
    f32 = jnp.float32
    out_gain = 0.5

    def w(shape, fan_in, gain=1.0):
        return jax.random.normal(next(ks), shape, f32) * (gain * fan_in ** -0.5)

    def ones_noise(shape):
        return 1.0 + 0.02 * jax.random.normal(next(ks), shape, f32)

    def small(shape, scale):
        return scale * jax.random.normal(next(ks), shape, f32)

    ne, no = N_EVEN, N_ODD
    x = jax.random.normal(next(ks), (BATCH, SEQ, D_MODEL), f32)
    mem = jax.random.normal(next(ks), (BATCH, N_MEM, D_MODEL), f32)
    offsets = jax.random.randint(next(ks), (BATCH, 1), 0, 64, dtype=jnp.int32) * CHUNK
    positions = (offsets + jnp.arange(SEQ, dtype=jnp.int32)[None, :]).astype(jnp.int32)

    norm_mix = ones_noise((DEPTH, D_MODEL))
    norm_x = ones_noise((DEPTH, D_MODEL))
    norm_mem = ones_noise((DEPTH, D_MODEL))
    x_wq = w((DEPTH, D_MODEL, X_HEADS * X_DH), D_MODEL)
    x_wkv = w((DEPTH, D_MODEL, 2 * X_HEADS * X_DH), D_MODEL)
    x_q_norm = ones_noise((DEPTH, X_DH))
    x_k_norm = ones_noise((DEPTH, X_DH))
    x_wo = w((DEPTH, X_HEADS * X_DH, D_MODEL), X_HEADS * X_DH, out_gain)
    norm_ffn = ones_noise((DEPTH, D_MODEL))
    ffn_w_in = w((DEPTH, D_MODEL, 2 * FF_HIDDEN), D_MODEL)
    ffn_w_out = w((DEPTH, FF_HIDDEN, D_MODEL), FF_HIDDEN, out_gain)

    ev_w_in = w((ne, D_MODEL, EV_COLS), D_MODEL)
    ev_conv_qkv = w((ne, SHORT_CONV, 2 * A_QK + A_V), SHORT_CONV)
    ev_a_log = jnp.log(jax.random.uniform(next(ks), (ne, A_HEADS), f32, 1.0, 16.0))
    dt = jnp.exp(jax.random.uniform(next(ks), (ne, A_HEADS), f32, math.log(1e-3), math.log(1e-1)))
    ev_dt_bias = dt + jnp.log(-jnp.expm1(-dt))
    ev_o_norm = ones_noise((ne, A_DV))
    ev_conv_b_w = w((ne, B_CONV, B_WIDTH), B_CONV)
    ev_conv_b_b = small((ne, B_WIDTH), 0.02)
    ev_gate_a_w = w((ne, B_BLOCKS, B_BLOCK_DIM, B_BLOCK_DIM), B_BLOCK_DIM)
    ev_gate_a_b = small((ne, B_WIDTH), 0.1)
    ev_gate_x_w = w((ne, B_BLOCKS, B_BLOCK_DIM, B_BLOCK_DIM), B_BLOCK_DIM)
    ev_gate_x_b = small((ne, B_WIDTH), 0.1)
    a0 = jax.random.uniform(next(ks), (ne, B_WIDTH), f32, 0.9, 0.999)
    a1 = a0 ** (1.0 / LRU_C)
    ev_lru_l = jnp.log(a1) - jnp.log1p(-a1)
    ev_w_out = w((ne, EV_OUT, D_MODEL), EV_OUT, out_gain)

    od_w_in = w((no, D_MODEL, OD_COLS), D_MODEL)
    od_c_q_norm = ones_noise((no, C_DK))
    od_c_k_norm = ones_noise((no, C_DK))
    od_lam_q1 = small((no, C_DK), 0.1)
    od_lam_k1 = small((no, C_DK), 0.1)
    od_lam_q2 = small((no, C_DK), 0.1)
    od_lam_k2 = small((no, C_DK), 0.1)
    od_c_sub_norm = ones_noise((no, C_DV))
    od_q_lat_norm = ones_noise((no, Q_LORA))
    od_w_uq = w((no, Q_LORA, D_HEADS * (D_NOPE + D_ROPE)), Q_LORA)
    od_kv_lat_norm = ones_noise((no, KV_LORA))
    od_w_ukv = w((no, KV_LORA, D_HEADS * (D_NOPE + D_DV)), KV_LORA)
    od_d_q_norm = ones_noise((no, D_NOPE + D_ROPE))
    od_d_k_norm = ones_noise((no, D_NOPE + D_ROPE))
    od_w_out = w((no, OD_OUT, D_MODEL), OD_OUT, out_gain)

    return {
        'x': x, 'mem': mem, 'positions': positions,
        'norm_mix': norm_mix, 'norm_x': norm_x, 'norm_mem': norm_mem,
        'x_wq': x_wq, 'x_wkv': x_wkv, 'x_q_norm': x_q_norm, 'x_k_norm': x_k_norm, 'x_wo': x_wo,
        'norm_ffn': norm_ffn, 'ffn_w_in': ffn_w_in, 'ffn_w_out': ffn_w_out,
        'ev_w_in': ev_w_in, 'ev_conv_qkv': ev_conv_qkv, 'ev_a_log': ev_a_log, 'ev_dt_bias': ev_dt_bias,
        'ev_o_norm': ev_o_norm, 'ev_conv_b_w': ev_conv_b_w, 'ev_conv_b_b': ev_conv_b_b,
        'ev_gate_a_w': ev_gate_a_w, 'ev_gate_a_b': ev_gate_a_b, 'ev_gate_x_w': ev_gate_x_w,
        'ev_gate_x_b': ev_gate_x_b, 'ev_lru_l': ev_lru_l, 'ev_w_out': ev_w_out,
        'od_w_in': od_w_in, 'od_c_q_norm': od_c_q_norm, 'od_c_k_norm': od_c_k_norm,
        'od_lam_q1': od_lam_q1, 'od_lam_k1': od_lam_k1, 'od_lam_q2': od_lam_q2, 'od_lam_k2': od_lam_k2,
        'od_c_sub_norm': od_c_sub_norm, 'od_q_lat_norm': od_q_lat_norm, 'od_w_uq': od_w_uq,
        'od_kv_lat_norm': od_kv_lat_norm, 'od_w_ukv': od_w_ukv, 'od_d_q_norm': od_d_q_norm,
        'od_d_k_norm': od_d_k_norm, 'od_w_out': od_w_out,
    }


def reference(x, mem, positions,
              norm_mix, norm_x, norm_mem, x_wq, x_wkv, x_q_norm, x_k_norm, x_wo,
              norm_ffn, ffn_w_in, ffn_w_out,
              ev_w_in, ev_conv_qkv, ev_a_log, ev_dt_bias, ev_o_norm, ev_conv_b_w, ev_conv_b_b,
              ev_gate_a_w, ev_gate_a_b, ev_gate_x_w, ev_gate_x_b, ev_lru_l, ev_w_out,
              od_w_in, od_c_q_norm, od_c_k_norm, od_lam_q1, od_lam_k1, od_lam_q2, od_lam_k2,
              od_c_sub_norm, od_q_lat_norm, od_w_uq, od_kv_lat_norm, od_w_ukv, od_d_q_norm,
              od_d_k_norm, od_w_out):
    for l in range(DEPTH):
        h = _rmsnorm(x, norm_mix[l])
        if l % 2 == 0:
            e = l // 2
            y = _even_mixer(h, ev_w_in[e], ev_conv_qkv[e], ev_a_log[e], ev_dt_bias[e], ev_o_norm[e],
                            ev_conv_b_w[e], ev_conv_b_b[e], ev_gate_a_w[e], ev_gate_a_b[e],
                            ev_gate_x_w[e], ev_gate_x_b[e], ev_lru_l[e], ev_w_out[e])
        else:
            o = l // 2
            y = _odd_mixer(h, positions, l, od_w_in[o], od_c_q_norm[o], od_c_k_norm[o],
                           od_lam_q1[o], od_lam_k1[o], od_lam_q2[o], od_lam_k2[o], od_c_sub_norm[o],
                           od_q_lat_norm[o], od_w_uq[o], od_kv_lat_norm[o], od_w_ukv[o],
                           od_d_q_norm[o], od_d_k_norm[o], od_w_out[o])
        x = x + y
        x = x + _memory_cross_attention(_rmsnorm(x, norm_x[l]), _rmsnorm(mem, norm_mem[l]),
                                        x_wq[l], x_wkv[l], x_q_norm[l], x_k_norm[l], x_wo[l])
        x = x + _swiglu(_rmsnorm(x, norm_ffn[l]), ffn_w_in[l], ffn_w_out[l])
    return x
```

```python
import functools
import math

import jax
import jax.numpy as jnp
import numpy as np
from jax import lax
from jax.experimental import pallas as pl
from jax.experimental.pallas import tpu as pltpu

F32 = jnp.float32
BF16 = jnp.bfloat16

NORM_EPS = 1e-6
NEG_INF = -1e30
CHUNK = 64
LANES = 128
LRU_C = 8.0
ROPE_THETA = 10000.0
VMEM_LIMIT = 56 * 1024 * 1024

A_HEADS = 4
A_D = 128
B_WIDTH = 512
B_BLOCKS = 8
C_HEADS = 4
C_DK = 64
D_HEADS = 4
D_NOPE = 64
D_ROPE = 32
D_QK = D_NOPE + D_ROPE
Q_LORA = 256
KV_LORA = 128
X_HEADS = 4
X_DH = 128


def _params(*sem):
    return pltpu.CompilerParams(dimension_semantics=sem, vmem_limit_bytes=VMEM_LIMIT)


def _rms(x, g):
    ms = jnp.mean(x * x, axis=-1, keepdims=True)
    return x * lax.rsqrt(ms + NORM_EPS) * g


def _dot(a, b):
    return jnp.dot(a, b, preferred_element_type=F32)


def _dot_nt(a, b):
    return lax.dot_general(a, b, (((1,), (1,)), ((), ())), preferred_element_type=F32)


def _sigmoid(x):
    return 1.0 / (1.0 + jnp.exp(-x))


def _silu(x):
    return x * _sigmoid(x)


def _softplus(x):
    return jnp.maximum(x, 0.0) + jnp.log(1.0 + jnp.exp(-jnp.abs(x)))


def _full(shape):
    return pl.BlockSpec(shape, lambda *_: (0,) * len(shape))


def _norm_proj_body(splits, x_ref, g_ref, w_ref, *out_refs):
    h = _rms(x_ref[...], g_ref[...]).astype(BF16)
    for (start, width), o_ref in zip(splits, out_refs):
        o_ref[...] = _dot(h, w_ref[:, start:start + width]).astype(o_ref.dtype)


def _norm_proj(x, g, w, widths, dtypes, tm):
    m, d = x.shape
    starts = np.concatenate([[0], np.cumsum(widths)[:-1]]).tolist()
    splits = tuple(zip(starts, widths))
    return pl.pallas_call(
        functools.partial(_norm_proj_body, splits),
        grid=(m // tm,),
        in_specs=[pl.BlockSpec((tm, d), lambda i: (i, 0)), _full(g.shape), _full(w.shape)],
        out_specs=[pl.BlockSpec((tm, wd), lambda i: (i, 0)) for wd in widths],
        out_shape=[jax.ShapeDtypeStruct((m, wd), dt) for wd, dt in zip(widths, dtypes)],
        compiler_params=_params("parallel"),
        name="norm_proj",
    )(x, g, w)


def _out_proj_body(ka, x_ref, a_ref, b_ref, w_ref, o_ref):
    y = _dot(a_ref[...], w_ref[:ka, :]) + _dot(b_ref[...], w_ref[ka:, :])
    o_ref[...] = x_ref[...] + y


def _out_proj(x, a, b, w, tm):
    m, d = x.shape
    ka, kb = a.shape[1], b.shape[1]
    return pl.pallas_call(
        functools.partial(_out_proj_body, ka),
        grid=(m // tm,),
        in_specs=[pl.BlockSpec((tm, d), lambda i: (i, 0)),
                  pl.BlockSpec((tm, ka), lambda i: (i, 0)),
                  pl.BlockSpec((tm, kb), lambda i: (i, 0)),
                  _full(w.shape)],
        out_specs=pl.BlockSpec((tm, d), lambda i: (i, 0)),
        out_shape=jax.ShapeDtypeStruct((m, d), F32),
        compiler_params=_params("parallel"),
        name="out_proj",
    )(x, a, b, w)


HALO = 8


def _causal_conv(x, w_ref, buf_ref, hist_ref, first):
    tb = x.shape[0]
    k = w_ref.shape[0]

    @pl.when(first)
    def _():
        hist_ref[...] = jnp.zeros_like(hist_ref)

    buf_ref[0:HALO, :] = hist_ref[...]
    buf_ref[HALO:, :] = x
    hist_ref[...] = x[tb - HALO:, :]
    y = buf_ref[pl.ds(HALO - (k - 1), tb), :] * w_ref[0:1, :]
    for j in range(1, k):
        y = y + buf_ref[pl.ds(HALO - (k - 1) + j, tb), :] * w_ref[j:j + 1, :]
    return y


def _delta_body(tb, qkv_ref, z_ref, bd_ref, cw_ref, avec_ref, dvec_ref, onorm_ref, tri_ref,
                y_ref, buf_ref, hist_ref, state_ref):
    first = pl.program_id(1) == 0
    nh, d = A_HEADS, A_D
    qk_w = nh * d

    @pl.when(first)
    def _():
        state_ref[...] = jnp.zeros_like(state_ref)

    act = _silu(_causal_conv(qkv_ref[...], cw_ref, buf_ref, hist_ref, first))

    bd = bd_ref[...]
    beta_all = _sigmoid(bd)
    g_all = -jnp.exp(avec_ref[...]) * _softplus(bd + dvec_ref[...])
    cum_all = jnp.dot(tri_ref[...], g_all, precision=lax.Precision.HIGHEST,
                      preferred_element_type=F32)

    cum_t = cum_all.T

    ri = lax.broadcasted_iota(jnp.int32, (CHUNK, CHUNK), 0)
    ci = lax.broadcasted_iota(jnp.int32, (CHUNK, CHUNK), 1)
    incl = ri >= ci
    strict = ri > ci
    eye = jnp.where(ri == ci, 1.0, 0.0).astype(F32)
    scale = d ** -0.5

    for h in range(nh):
        q_h = act[:, h * d:(h + 1) * d]
        k_h = act[:, qk_w + h * d:qk_w + (h + 1) * d]
        v_h = act[:, 2 * qk_w + h * d:2 * qk_w + (h + 1) * d]
        q_h = q_h * (lax.rsqrt(jnp.sum(q_h * q_h, axis=-1, keepdims=True) + NORM_EPS) * scale)
        k_h = k_h * lax.rsqrt(jnp.sum(k_h * k_h, axis=-1, keepdims=True) + NORM_EPS)
        state = state_ref[h]
        outs = []
        for c in range(tb // CHUNK):
            rows = slice(c * CHUNK, (c + 1) * CHUNK)
            q = q_h[rows]
            k = k_h[rows]
            v = v_h[rows]
            beta = beta_all[rows, h:h + 1]
            cum_blk = cum_all[rows, :]
            cum = cum_blk[:, nh + h:nh + h + 1]
            cum_row = cum_t[nh + h:nh + h + 1, c * CHUNK:(c + 1) * CHUNK]
            cum_last = cum[CHUNK - 1:CHUNK, :]
            decay = jnp.where(incl, jnp.exp(jnp.where(incl, cum - cum_row, 0.0)), 0.0)
            e_cum = jnp.exp(cum)
            kb = k * beta
            k_bf = k.astype(BF16)
            a = _dot_nt(jnp.concatenate([q, kb], axis=0).astype(BF16), k_bf)
            qk = jnp.where(incl, a[:CHUNK] * decay, 0.0)
            m = jnp.where(strict, a[CHUNK:] * decay, 0.0)
            p = -m
            t_inv = eye + p
            for _ in range(CHUNK.bit_length() - 2):
                p_bf = p.astype(BF16)
                p = _dot(p_bf, p_bf)
                t_inv = _dot(t_inv.astype(BF16), (eye + p).astype(BF16))
            wu = _dot(t_inv.astype(BF16),
                      jnp.concatenate([kb * e_cum, v * beta], axis=1).astype(BF16))
            w = wu[:, :d]
            u = wu[:, d:]
            q_dec = q * e_cum
            k_dec = k * jnp.exp(cum_last - cum)
            ws = _dot(jnp.concatenate([w, q_dec], axis=0).astype(BF16), state.astype(BF16))
            v_new = u - ws[:CHUNK]
            v_new_bf = v_new.astype(BF16)
            outs.append(ws[CHUNK:] + _dot(qk.astype(BF16), v_new_bf))
            state = state * jnp.exp(cum_last) + _dot(k_dec.T.astype(BF16), v_new_bf)
        state_ref[h] = state
        o = jnp.concatenate(outs, axis=0)
        o = _rms(o, onorm_ref[...]) * _silu(z_ref[:, h * d:(h + 1) * d])
        y_ref[:, h * d:(h + 1) * d] = o.astype(y_ref.dtype)


def _delta_rule(qkv, z, bd, conv_w, avec, dvec, onorm, batch, tb):
    m, c3 = qkv.shape
    s = m // batch
    nsb = s // tb
    cid = np.arange(tb) // CHUNK
    tri = jnp.asarray(((cid[:, None] == cid[None, :]) &
                       (np.arange(tb)[:, None] >= np.arange(tb)[None, :])).astype(np.float32))
    vw = A_HEADS * A_D
    row = lambda b, j: (b * nsb + j, 0)
    return pl.pallas_call(
        functools.partial(_delta_body, tb),
        grid=(batch, nsb),
        in_specs=[pl.BlockSpec((tb, c3), row), pl.BlockSpec((tb, vw), row),
                  pl.BlockSpec((tb, LANES), row), _full(conv_w.shape), _full(avec.shape),
                  _full(dvec.shape), _full(onorm.shape), _full(tri.shape)],
        out_specs=pl.BlockSpec((tb, vw), row),
        out_shape=jax.ShapeDtypeStruct((m, vw), BF16),
        scratch_shapes=[pltpu.VMEM((HALO + tb, c3), F32), pltpu.VMEM((HALO, c3), F32),
                        pltpu.VMEM((A_HEADS, A_D, A_D), F32)],
        compiler_params=_params("parallel", "arbitrary"),
        name="delta_rule",
    )(qkv, z, bd, conv_w, avec, dvec, onorm, tri)


def _gelu_tanh(x):
    return 0.5 * x * (1.0 + jnp.tanh(math.sqrt(2.0 / math.pi) * (x + 0.044715 * (x * x * x))))


def _lru_body(tb, xb_ref, gb_ref, cw_ref, cb_ref, wa_ref, ba_ref, wx_ref, bx_ref, l_ref,
              y_ref, buf_ref, hist_ref, h_ref):
    first = pl.program_id(1) == 0

    @pl.when(first)
    def _():
        h_ref[...] = jnp.zeros_like(h_ref)

    xc = _causal_conv(xb_ref[...], cw_ref, buf_ref, hist_ref, first) + cb_ref[...]
    xc_bf = xc.astype(BF16)
    r = _sigmoid(_dot(xc_bf, wa_ref[...]) + ba_ref[...])
    i = _sigmoid(_dot(xc_bf, wx_ref[...]) + bx_ref[...])
    log_a = (-LRU_C) * r * _softplus(-l_ref[...])
    a = jnp.exp(log_a)
    th = jnp.tanh(log_a)
    u = jnp.sqrt(-2.0 * th / (1.0 - th)) * (i * xc)
    rows = lax.broadcasted_iota(jnp.int32, a.shape, 0)
    shift = 1
    while shift < tb:
        keep = rows >= shift
        a_prev = jnp.where(keep, pltpu.roll(a, shift, 0), 1.0)
        u_prev = jnp.where(keep, pltpu.roll(u, shift, 0), 0.0)
        u = u + a * u_prev
        a = a * a_prev
        shift *= 2
    hs = u + a * h_ref[...]
    h_ref[...] = hs[tb - 1:tb, :]
    y_ref[...] = (_gelu_tanh(gb_ref[...]) * hs).astype(y_ref.dtype)


def _rglru(xb, gb, cw, cb, wa, ba, wx, bx, lru_l, batch, tb):
    m, wdt = xb.shape
    nsb = (m // batch) // tb
    row = lambda b, j: (b * nsb + j, 0)
    return pl.pallas_call(
        functools.partial(_lru_body, tb),
        grid=(batch, nsb),
        in_specs=[pl.BlockSpec((tb, wdt), row), pl.BlockSpec((tb, wdt), row),
                  _full(cw.shape), _full(cb.shape), _full(wa.shape), _full(ba.shape),
                  _full(wx.shape), _full(bx.shape), _full(lru_l.shape)],
        out_specs=pl.BlockSpec((tb, wdt), row),
        out_shape=jax.ShapeDtypeStruct((m, wdt), BF16),
        scratch_shapes=[pltpu.VMEM((HALO + tb, wdt), F32), pltpu.VMEM((HALO, wdt), F32),
                        pltpu.VMEM((1, wdt), F32)],
        compiler_params=_params("parallel", "arbitrary"),
        name="rglru",
    )(xb, gb, cw, cb, wa, ba, wx, bx, lru_l)


def _head_norm(x, nblk, denom):
    outs = []
    for h in range(nblk):
        xh = x[:, h * LANES:(h + 1) * LANES]
        ss = jnp.sum(xh * xh, axis=-1, keepdims=True)
        outs.append(xh * lax.rsqrt(ss * (1.0 / denom) + NORM_EPS))
    return jnp.concatenate(outs, axis=1)


def _half_norm(x, nblk):
    lane = lax.broadcasted_iota(jnp.int32, (1, LANES), 1)
    lo = lane < C_DK
    outs = []
    for h in range(nblk):
        xh = x[:, h * LANES:(h + 1) * LANES]
        sq = xh * xh
        s_lo = jnp.sum(jnp.where(lo, sq, 0.0), axis=-1, keepdims=True)
        s_hi = jnp.sum(jnp.where(lo, 0.0, sq), axis=-1, keepdims=True)
        inv = jnp.where(lo, lax.rsqrt(s_lo * (1.0 / C_DK) + NORM_EPS),
                        lax.rsqrt(s_hi * (1.0 / C_DK) + NORM_EPS))
        outs.append(xh * inv)
    return jnp.concatenate(outs, axis=1)


def _rope_blocks(x, nblk, cos, sin):
    lane = lax.broadcasted_iota(jnp.int32, (1, LANES), 1)
    first_half = lane < D_NOPE + D_ROPE // 2
    outs = []
    for h in range(nblk):
        xh = x[:, h * LANES:(h + 1) * LANES]
        partner = jnp.where(first_half, pltpu.roll(xh, LANES - D_ROPE // 2, 1),
                            pltpu.roll(xh, D_ROPE // 2, 1))
        outs.append(xh * cos + partner * sin)
    return jnp.concatenate(outs, axis=1)


def _odd_in_body(x_ref, pos_ref, g_ref, w_ref, cqn_ref, ckn_ref, qln_ref, wuq_ref, kvn_ref,
                 wukv_ref, dqn_ref, dkn_ref, freq_ref,
                 qc_ref, kc_ref, vc_ref, qd_ref, kd_ref, vd_ref):
    hw = C_HEADS * LANES
    h = _rms(x_ref[...], g_ref[...]).astype(BF16)
    qc = _dot(h, w_ref[:, 0:hw])
    qc_ref[...] = (_half_norm(qc, C_HEADS) * (cqn_ref[...] * C_DK ** -0.5)).astype(BF16)
    kc = _dot(h, w_ref[:, hw:2 * hw])
    kc_ref[...] = (_half_norm(kc, C_HEADS) * ckn_ref[...]).astype(BF16)
    vc_ref[...] = _dot(h, w_ref[:, 2 * hw:3 * hw]).astype(BF16)

    lane = lax.broadcasted_iota(jnp.int32, (1, LANES), 1)
    ang = pos_ref[...].astype(F32) * freq_ref[...]
    is_rope = (lane >= D_NOPE) & (lane < D_QK)
    cos = jnp.where(is_rope, jnp.cos(ang), 1.0)
    sin_raw = jnp.sin(ang)
    sin = jnp.where(is_rope, jnp.where(lane < D_NOPE + D_ROPE // 2, -sin_raw, sin_raw), 0.0)

    off = 3 * hw
    q_lat = _dot(h, w_ref[:, off:off + Q_LORA])
    qd = _dot(_rms(q_lat, qln_ref[...]).astype(BF16), wuq_ref[...])
    qd = _head_norm(qd, D_HEADS, D_QK) * (dqn_ref[...] * D_QK ** -0.5)
    qd_ref[...] = _rope_blocks(qd, D_HEADS, cos, sin).astype(BF16)

    off += Q_LORA
    kv_lat = _dot(h, w_ref[:, off:off + KV_LORA])
    k_rope = _dot(h, w_ref[:, off + KV_LORA:off + KV_LORA + LANES])
    kvd = _dot(_rms(kv_lat, kvn_ref[...]).astype(BF16), wukv_ref[...])
    dhw = D_HEADS * LANES
    kd = kvd[:, :dhw] + jnp.concatenate([k_rope] * D_HEADS, axis=1)
    kd = _head_norm(kd, D_HEADS, D_QK) * dkn_ref[...]
    kd_ref[...] = _rope_blocks(kd, D_HEADS, cos, sin).astype(BF16)
    vd_ref[...] = kvd[:, dhw:].astype(BF16)


def _odd_in(x, pos, g, w, cqn, ckn, qln, wuq, kvn, wukv, dqn, dkn, freq, tm):
    m, d = x.shape
    hw = C_HEADS * LANES
    row = lambda i: (i, 0)
    small = [g, w, cqn, ckn, qln, wuq, kvn, wukv, dqn, dkn, freq]
    return pl.pallas_call(
        _odd_in_body,
        grid=(m // tm,),
        in_specs=[pl.BlockSpec((tm, d), row), pl.BlockSpec((tm, 1), row)] + [_full(a.shape) for a in small],
        out_specs=[pl.BlockSpec((tm, hw), row)] * 6,
        out_shape=[jax.ShapeDtypeStruct((m, hw), BF16)] * 6,
        compiler_params=_params("parallel"),
        name="odd_in",
    )(x, pos, *small)


def _flash_body(n_maps, alibi, t, lam_init, *refs):
    if alibi:
        lam_ref, sub_ref, q_ref, k_ref, v_ref, pq_ref, pk_ref, o_ref = refs
    else:
        q_ref, k_ref, v_ref, o_ref = refs
    head = pl.program_id(1)
    qi = pl.program_id(2)
    q = q_ref[...]
    if n_maps == 2:
        lane = lax.broadcasted_iota(jnp.int32, (1, LANES), 1)
        zero = jnp.zeros_like(q)
        q = jnp.concatenate([jnp.where(lane < C_DK, q, zero), jnp.where(lane < C_DK, zero, q)], axis=0)
    rows = n_maps * t
    if alibi:
        slope = jnp.exp2(-2.0 * (jnp.zeros((1, 1), F32) + (head + 1).astype(F32)))
        pq = pq_ref[...].astype(F32)

    def scores(kb):
        kblk = k_ref[pl.ds(pl.multiple_of(kb * t, t), t), :]
        s = _dot_nt(q, kblk)
        if alibi:
            bias = slope * jnp.abs(pq - pk_ref[0, kb].astype(F32))
            s = s - jnp.concatenate([bias] * n_maps, axis=0)
        return s

    def update(kb, s, carry):
        m_i, l_i, acc = carry
        vblk = v_ref[pl.ds(pl.multiple_of(kb * t, t), t), :]
        m_new = jnp.maximum(m_i, jnp.max(s, axis=-1, keepdims=True))
        alpha = jnp.exp(m_i - m_new)
        p = jnp.exp(s - m_new)
        l_new = alpha * l_i + jnp.sum(p, axis=-1, keepdims=True)
        acc_new = alpha * acc + _dot(p.astype(BF16), vblk)
        return m_new, l_new, acc_new

    init = (jnp.full((rows, 1), NEG_INF, F32), jnp.zeros((rows, 1), F32), jnp.zeros((rows, LANES), F32))
    carry = lax.fori_loop(0, qi, lambda kb, c: update(kb, scores(kb), c), init)
    ri = lax.broadcasted_iota(jnp.int32, (rows, t), 0)
    ci = lax.broadcasted_iota(jnp.int32, (rows, t), 1)
    shift = CHUNK.bit_length() - 1
    allowed = (ci >> shift) <= ((ri & (t - 1)) >> shift)
    s = jnp.where(allowed, scores(qi), NEG_INF)
    _, l_i, acc = update(qi, s, carry)
    o = acc / l_i
    if n_maps == 2:
        lv = lam_ref[...]
        lam = (jnp.exp(jnp.sum(lv[0:1] * lv[1:2], axis=-1, keepdims=True))
               - jnp.exp(jnp.sum(lv[2:3] * lv[3:4], axis=-1, keepdims=True)) + lam_init)
        o = o[:t] - lam * o[t:]
        o = _rms(o, sub_ref[...]) * (1.0 - lam_init)
    o_ref[...] = o.astype(o_ref.dtype)


def _flash(q, k, v, batch, t, n_maps, extras=None, lam_init=0.0):
    m, hw = q.shape
    s = m // batch
    nq = s // t
    nh = hw // LANES
    alibi = extras is not None
    qspec = pl.BlockSpec((t, LANES), lambda b, h, i: (b * nq + i, h))
    kvspec = pl.BlockSpec((s, LANES), lambda b, h, i: (b, h))
    in_specs = [qspec, kvspec, kvspec]
    args = [q, k, v]
    if alibi:
        lamv, subn, pos_col, pos_row = extras
        in_specs = [_full(lamv.shape), _full(subn.shape)] + in_specs + [
            pl.BlockSpec((t, 1), lambda b, h, i: (b * nq + i, 0)),
            pl.BlockSpec((1, nq, 1, t), lambda b, h, i: (b, 0, 0, 0))]
        args = [lamv, subn] + args + [pos_col, pos_row]
    return pl.pallas_call(
        functools.partial(_flash_body, n_maps, alibi, t, lam_init),
        grid=(batch, nh, nq),
        in_specs=in_specs,
        out_specs=qspec,
        out_shape=jax.ShapeDtypeStruct((m, hw), BF16),
        compiler_params=_params("parallel", "parallel", "arbitrary"),
        name="flash_diff" if alibi else "flash_mla",
    )(*args)


def _mem_kv_body(mem_ref, g_ref, w_ref, kn_ref, k_ref, v_ref):
    hw = X_HEADS * X_DH
    mn = _rms(mem_ref[...], g_ref[...]).astype(BF16)
    k = _dot(mn, w_ref[:, :hw])
    k_ref[...] = (_head_norm(k, X_HEADS, X_DH) * kn_ref[...]).astype(BF16)
    v_ref[...] = _dot(mn, w_ref[:, hw:]).astype(BF16)


def _mem_kv(mem2, g, w, kn, batch):
    rows, d = mem2.shape
    nm = rows // batch
    hw = X_HEADS * X_DH
    return pl.pallas_call(
        _mem_kv_body,
        grid=(batch,),
        in_specs=[pl.BlockSpec((nm, d), lambda b: (b, 0)), _full(g.shape), _full(w.shape), _full(kn.shape)],
        out_specs=[pl.BlockSpec((nm, hw), lambda b: (b, 0))] * 2,
        out_shape=[jax.ShapeDtypeStruct((rows, hw), BF16)] * 2,
        compiler_params=_params("parallel"),
        name="mem_kv",
    )(mem2, g, w, kn)


def _cross_body(x_ref, g_ref, wq_ref, qn_ref, k_ref, v_ref, wo_ref, o_ref):
    x = x_ref[...]
    q = _dot(_rms(x, g_ref[...]).astype(BF16), wq_ref[...])
    q = (_head_norm(q, X_HEADS, X_DH) * (qn_ref[...] * X_DH ** -0.5)).astype(BF16)
    outs = []
    for h in range(X_HEADS):
        blk = slice(h * X_DH, (h + 1) * X_DH)
        s = _dot_nt(q[:, blk], k_ref[:, blk])
        p = jnp.exp(s - jnp.max(s, axis=-1, keepdims=True))
        l = jnp.sum(p, axis=-1, keepdims=True)
        outs.append((_dot(p.astype(BF16), v_ref[:, blk]) / l).astype(BF16))
    o_ref[...] = x + _dot(jnp.concatenate(outs, axis=1), wo_ref[...])


def _cross_attn(x, g, wq, qn, k, v, wo, batch, tm):
    m, d = x.shape
    per = (m // batch) // tm
    nm = k.shape[0] // batch
    hw = X_HEADS * X_DH
    kv = pl.BlockSpec((nm, hw), lambda i: (i // per, 0))
    return pl.pallas_call(
        _cross_body,
        grid=(m // tm,),
        in_specs=[pl.BlockSpec((tm, d), lambda i: (i, 0)), _full(g.shape), _full(wq.shape),
                  _full(qn.shape), kv, kv, _full(wo.shape)],
        out_specs=pl.BlockSpec((tm, d), lambda i: (i, 0)),
        out_shape=jax.ShapeDtypeStruct((m, d), F32),
        compiler_params=_params("parallel"),
        name="cross_attn",
    )(x, g, wq, qn, k, v, wo)


def _ffn_body(hid, th, x_ref, g_ref, wi_ref, wo_ref, o_ref):
    x = x_ref[...]
    h = _rms(x, g_ref[...]).astype(BF16)
    acc = x
    for c in range(hid // th):
        gate = _dot(h, wi_ref[:, c * th:(c + 1) * th])
        up = _dot(h, wi_ref[:, hid + c * th:hid + (c + 1) * th])
        acc = acc + _dot((_silu(gate) * up).astype(BF16), wo_ref[c * th:(c + 1) * th, :])
    o_ref[...] = acc


def _ffn(x, g, wi, wo, tm, th):
    m, d = x.shape
    hid = wo.shape[0]
    return pl.pallas_call(
        functools.partial(_ffn_body, hid, th),
        grid=(m // tm,),
        in_specs=[pl.BlockSpec((tm, d), lambda i: (i, 0)), _full(g.shape),
                  pl.BlockSpec(wi.shape, lambda i: (0, 0), pipeline_mode=pl.Buffered(1)),
                  pl.BlockSpec(wo.shape, lambda i: (0, 0), pipeline_mode=pl.Buffered(1))],
        out_specs=pl.BlockSpec((tm, d), lambda i: (i, 0)),
        out_shape=jax.ShapeDtypeStruct((m, d), F32),
        compiler_params=_params("parallel"),
        name="ffn",
    )(x, g, wi, wo)


def _row(v):
    return v.reshape(1, -1).astype(F32)


def _pad_cols(w, width):
    return jnp.pad(w, ((0, 0), (0, width - w.shape[1])))


def _block_diag(w):
    n, d, e = w.shape
    eye = jnp.eye(n, dtype=w.dtype)
    return (eye[:, None, :, None] * w[:, :, None, :]).reshape(n * d, n * e)


def _even_layer(x, batch, tm, tb, norm_g, w_in, conv_qkv, a_log, dt_bias, o_norm, conv_b_w, conv_b_b,
                gate_a_w, gate_a_b, gate_x_w, gate_x_b, lru_l, w_out):
    qk = A_HEADS * A_D
    c_qkv, c_z = 3 * qk, qk
    o = 0
    w_qkv = w_in[:, o:o + c_qkv]; o += c_qkv
    w_z = w_in[:, o:o + c_z]; o += c_z
    w_bd = w_in[:, o:o + 2 * A_HEADS]; o += 2 * A_HEADS
    w_xb = w_in[:, o:o + B_WIDTH]; o += B_WIDTH
    w_gb = w_in[:, o:o + B_WIDTH]
    w = jnp.concatenate([w_qkv, w_z, w_xb, w_gb, _pad_cols(w_bd, LANES)], axis=1).astype(BF16)
    qkv, z, xb, gb, bd = _norm_proj(x, _row(norm_g), w, [c_qkv, c_z, B_WIDTH, B_WIDTH, LANES],
                                    [F32] * 5, tm)
    pad = jnp.zeros((A_HEADS,), F32)
    avec = _row(jnp.pad(jnp.concatenate([pad, a_log.astype(F32)]), (0, LANES - 2 * A_HEADS)))
    dvec = _row(jnp.pad(jnp.concatenate([pad, dt_bias.astype(F32)]), (0, LANES - 2 * A_HEADS)))
    y_a = _delta_rule(qkv, z, bd, conv_qkv.astype(F32), avec, dvec, _row(o_norm), batch, tb)
    y_b = _rglru(xb, gb, conv_b_w.astype(F32), _row(conv_b_b), _block_diag(gate_a_w).astype(BF16),
                 _row(gate_a_b), _block_diag(gate_x_w).astype(BF16), _row(gate_x_b), _row(lru_l),
                 batch, tb)
    return _out_proj(x, y_a, y_b, w_out.astype(BF16), tm)


def _head_pad_cols(w, heads, real, total):
    k = w.shape[0]
    return jnp.pad(w.reshape(k, heads, real), ((0, 0), (0, 0), (0, total - real))).reshape(k, heads * total)


def _odd_layer(x, pos_col, pos_row, batch, tm, tq, layer_idx, norm_g, w_in, c_q_norm, c_k_norm,
               lam_q1, lam_k1, lam_q2, lam_k2, c_sub_norm, q_lat_norm, w_uq, kv_lat_norm, w_ukv,
               d_q_norm, d_k_norm, w_out):
    hw = C_HEADS * 2 * C_DK
    k_rope_w = jnp.pad(w_in[:, 3 * hw + Q_LORA + KV_LORA:], ((0, 0), (D_NOPE, LANES - D_QK)))
    w = jnp.concatenate([w_in[:, :3 * hw + Q_LORA + KV_LORA], k_rope_w], axis=1).astype(BF16)
    wuq = _head_pad_cols(w_uq, D_HEADS, D_QK, LANES).astype(BF16)
    kvr = w_ukv.reshape(KV_LORA, D_HEADS, D_NOPE + LANES)
    wukv = jnp.concatenate([_head_pad_cols(kvr[:, :, :D_NOPE].reshape(KV_LORA, -1), D_HEADS, D_NOPE, LANES),
                            kvr[:, :, D_NOPE:].reshape(KV_LORA, -1)], axis=1).astype(BF16)
    tile = lambda v, n: _row(jnp.tile(v.astype(F32), n))
    padn = lambda v: jnp.pad(v.astype(F32), (0, LANES - D_QK))
    half = D_ROPE // 2
    inv_freq = ROPE_THETA ** (-jnp.arange(half, dtype=F32) / half)
    freq = _row(jnp.concatenate([jnp.zeros((D_NOPE,), F32), inv_freq, inv_freq,
                                 jnp.zeros((LANES - D_QK,), F32)]))
    qc, kc, vc, qd, kd, vd = _odd_in(
        x, pos_col, _row(norm_g), w, tile(c_q_norm, 2 * C_HEADS), tile(c_k_norm, 2 * C_HEADS),
        _row(q_lat_norm), wuq, _row(kv_lat_norm), wukv, tile(padn(d_q_norm), D_HEADS),
        tile(padn(d_k_norm), D_HEADS), freq, tm)
    lam_init = 0.8 - 0.6 * math.exp(-0.3 * layer_idx)
    lamv = jnp.stack([lam_q1, lam_k1, lam_q2, lam_k2]).astype(F32)
    yc = _flash(qc, kc, vc, batch, tq, 2, (lamv, _row(c_sub_norm), pos_col, pos_row), lam_init)
    yd = _flash(qd, kd, vd, batch, tq, 1)
    return _out_proj(x, yc, yd, w_out.astype(BF16), tm)


def kernel(x, mem, positions, norm_mix, norm_x, norm_mem, x_wq, x_wkv, x_q_norm, x_k_norm, x_wo, norm_ffn, ffn_w_in, ffn_w_out, ev_w_in, ev_conv_qkv, ev_a_log, ev_dt_bias, ev_o_norm, ev_conv_b_w, ev_conv_b_b, ev_gate_a_w, ev_gate_a_b, ev_gate_x_w, ev_gate_x_b, ev_lru_l, ev_w_out, od_w_in, od_c_q_norm, od_c_k_norm, od_lam_q1, od_lam_k1, od_lam_q2, od_lam_k2, od_c_sub_norm, od_q_lat_norm, od_w_uq, od_kv_lat_norm, od_w_ukv, od_d_q_norm, od_d_k_norm, od_w_out):
    batch, seq, d = x.shape
    depth = norm_mix.shape[0]
    m = batch * seq
    tm = min(512, seq)
    tb = min(256, seq)
    tq = min(256, seq)
    th = ffn_w_out.shape[1]
    th = 1408 if th % 1408 == 0 else th
    xf = x.reshape(m, d).astype(F32)
    mem2 = mem.reshape(-1, d).astype(F32)
    pos_col = positions.reshape(m, 1).astype(jnp.int32)
    pos_row = positions.reshape(batch, seq // tq, 1, tq).astype(jnp.int32)
    for l in range(depth):
        i = l // 2
        if l % 2 == 0:
            xf = _even_layer(xf, batch, tm, tb, norm_mix[l], ev_w_in[i], ev_conv_qkv[i], ev_a_log[i],
                             ev_dt_bias[i], ev_o_norm[i], ev_conv_b_w[i], ev_conv_b_b[i], ev_gate_a_w[i],
                             ev_gate_a_b[i], ev_gate_x_w[i], ev_gate_x_b[i], ev_lru_l[i], ev_w_out[i])
        else:
            xf = _odd_layer(xf, pos_col, pos_row, batch, tm, tq, l, norm_mix[l], od_w_in[i],
                            od_c_q_norm[i], od_c_k_norm[i], od_lam_q1[i], od_lam_k1[i], od_lam_q2[i],
                            od_lam_k2[i], od_c_sub_norm[i], od_q_lat_norm[i], od_w_uq[i],
                            od_kv_lat_norm[i], od_w_ukv[i], od_d_q_norm[i], od_d_k_norm[i], od_w_out[i])
        k_mem, v_mem = _mem_kv(mem2, _row(norm_mem[l]), x_wkv[l].astype(BF16),
                               _row(jnp.tile(x_k_norm[l], X_HEADS)), batch)
        xf = _cross_attn(xf, _row(norm_x[l]), x_wq[l].astype(BF16), _row(jnp.tile(x_q_norm[l], X_HEADS)),
                         k_mem, v_mem, x_wo[l].astype(BF16), batch, tm)
        xf = _ffn(xf, _row(norm_ffn[l]), ffn_w_in[l].astype(BF16), ffn_w_out[l].astype(BF16), tm, th)
    return xf.reshape(batch, seq, d).astype(x.dtype)
```

```python
import functools
import math

import jax
import jax.numpy as jnp
import numpy as np
from jax import lax
from jax.experimental import pallas as pl
from jax.experimental.pallas import tpu as pltpu

F32 = jnp.float32
BF16 = jnp.bfloat16

NORM_EPS = 1e-6
NEG_INF = -1e30
CHUNK = 64
LANES = 128
LRU_C = 8.0
ROPE_THETA = 10000.0
VMEM_LIMIT = 56 * 1024 * 1024

A_HEADS = 4
A_D = 128
B_WIDTH = 512
B_BLOCKS = 8
C_HEADS = 4
C_DK = 64
D_HEADS = 4
D_NOPE = 64
D_ROPE = 32
D_QK = D_NOPE + D_ROPE
Q_LORA = 256
KV_LORA = 128
X_HEADS = 4
X_DH = 128


def _params(*sem):
    return pltpu.CompilerParams(dimension_semantics=sem, vmem_limit_bytes=VMEM_LIMIT)


def _rms(x, g):
    ms = jnp.mean(x * x, axis=-1, keepdims=True)
    return x * lax.rsqrt(ms + NORM_EPS) * g


def _dot(a, b):
    return jnp.dot(a, b, preferred_element_type=F32)


def _dot_nt(a, b):
    return lax.dot_general(a, b, (((1,), (1,)), ((), ())), preferred_element_type=F32)


def _sigmoid(x):
    return 1.0 / (1.0 + jnp.exp(-x))


def _silu(x):
    return x * _sigmoid(x)


def _softplus(x):
    return jnp.maximum(x, 0.0) + jnp.log(1.0 + jnp.exp(-jnp.abs(x)))


def _full(shape):
    return pl.BlockSpec(shape, lambda *_: (0,) * len(shape))


def _norm_proj_body(splits, x_ref, g_ref, w_ref, *out_refs):
    h = _rms(x_ref[...], g_ref[...]).astype(BF16)
    for (start, width), o_ref in zip(splits, out_refs):
        o_ref[...] = _dot(h, w_ref[:, start:start + width]).astype(o_ref.dtype)


def _norm_proj(x, g, w, widths, dtypes, tm):
    m, d = x.shape
    starts = np.concatenate([[0], np.cumsum(widths)[:-1]]).tolist()
    splits = tuple(zip(starts, widths))
    return pl.pallas_call(
        functools.partial(_norm_proj_body, splits),
        grid=(m // tm,),
        in_specs=[pl.BlockSpec((tm, d), lambda i: (i, 0)), _full(g.shape), _full(w.shape)],
        out_specs=[pl.BlockSpec((tm, wd), lambda i: (i, 0)) for wd in widths],
        out_shape=[jax.ShapeDtypeStruct((m, wd), dt) for wd, dt in zip(widths, dtypes)],
        compiler_params=_params("parallel"),
        name="norm_proj",
    )(x, g, w)


def _out_proj_body(ka, x_ref, a_ref, b_ref, w_ref, o_ref):
    y = _dot(a_ref[...], w_ref[:ka, :]) + _dot(b_ref[...], w_ref[ka:, :])
    o_ref[...] = x_ref[...] + y


def _out_proj(x, a, b, w, tm):
    m, d = x.shape
    ka, kb = a.shape[1], b.shape[1]
    return pl.pallas_call(
        functools.partial(_out_proj_body, ka),
        grid=(m // tm,),
        in_specs=[pl.BlockSpec((tm, d), lambda i: (i, 0)),
                  pl.BlockSpec((tm, ka), lambda i: (i, 0)),
                  pl.BlockSpec((tm, kb), lambda i: (i, 0)),
                  _full(w.shape)],
        out_specs=pl.BlockSpec((tm, d), lambda i: (i, 0)),
        out_shape=jax.ShapeDtypeStruct((m, d), F32),
        compiler_params=_params("parallel"),
        name="out_proj",
    )(x, a, b, w)


HALO = 8


def _causal_conv(x, w_ref, buf_ref, hist_ref, first):
    tb = x.shape[0]
    k = w_ref.shape[0]

    @pl.when(first)
    def _():
        hist_ref[...] = jnp.zeros_like(hist_ref)

    buf_ref[0:HALO, :] = hist_ref[...]
    buf_ref[HALO:, :] = x
    hist_ref[...] = x[tb - HALO:, :]
    y = buf_ref[pl.ds(HALO - (k - 1), tb), :] * w_ref[0:1, :]
    for j in range(1, k):
        y = y + buf_ref[pl.ds(HALO - (k - 1) + j, tb), :] * w_ref[j:j + 1, :]
    return y


def _delta_body(tb, qkv_ref, z_ref, bd_ref, cw_ref, avec_ref, dvec_ref, onorm_ref, tri_ref,
                y_ref, buf_ref, hist_ref, state_ref):
    first = pl.program_id(1) == 0
    nh, d = A_HEADS, A_D
    qk_w = nh * d

    @pl.when(first)
    def _():
        state_ref[...] = jnp.zeros_like(state_ref)

    act = _silu(_causal_conv(qkv_ref[...], cw_ref, buf_ref, hist_ref, first))

    bd = bd_ref[...]
    beta_all = _sigmoid(bd)
    g_all = -jnp.exp(avec_ref[...]) * _softplus(bd + dvec_ref[...])
    cum_all = jnp.dot(tri_ref[...], g_all, precision=lax.Precision.HIGHEST,
                      preferred_element_type=F32)

    cum_t = cum_all.T

    ri = lax.broadcasted_iota(jnp.int32, (CHUNK, CHUNK), 0)
    ci = lax.broadcasted_iota(jnp.int32, (CHUNK, CHUNK), 1)
    incl = ri >= ci
    strict = ri > ci
    eye = jnp.where(ri == ci, 1.0, 0.0).astype(F32)
    scale = d ** -0.5

    nc = tb // CHUNK
    probs = [(h, c) for c in range(nc) for h in range(nh)]
    qs, ks, vs = {}, {}, {}
    for h in range(nh):
        q_h = act[:, h * d:(h + 1) * d]
        k_h = act[:, qk_w + h * d:qk_w + (h + 1) * d]
        q_h = q_h * (lax.rsqrt(jnp.sum(q_h * q_h, axis=-1, keepdims=True) + NORM_EPS) * scale)
        k_h = k_h * lax.rsqrt(jnp.sum(k_h * k_h, axis=-1, keepdims=True) + NORM_EPS)
        for c in range(nc):
            rows = slice(c * CHUNK, (c + 1) * CHUNK)
            qs[h, c] = q_h[rows]
            ks[h, c] = k_h[rows]
            vs[h, c] = act[rows, 2 * qk_w + h * d:2 * qk_w + (h + 1) * d]

    a_mat, rhs_wu, q_dec, k_dec_t, qk, neg_m, last = {}, {}, {}, {}, {}, {}, {}
    for pr in probs:
        h, c = pr
        rows = slice(c * CHUNK, (c + 1) * CHUNK)
        q, k, v = qs[pr], ks[pr], vs[pr]
        beta = beta_all[rows, h:h + 1]
        cum = cum_all[rows, nh + h:nh + h + 1]
        cum_row = cum_t[nh + h:nh + h + 1, c * CHUNK:(c + 1) * CHUNK]
        cum_last = cum[CHUNK - 1:CHUNK, :]
        decay = jnp.where(incl, jnp.exp(jnp.where(incl, cum - cum_row, 0.0)), 0.0)
        e_cum = jnp.exp(cum)
        kb = k * beta
        a_mat[pr] = _dot_nt(jnp.concatenate([q, kb], axis=0).astype(BF16), k.astype(BF16))
        qk[pr] = jnp.where(incl, a_mat[pr][:CHUNK] * decay, 0.0).astype(BF16)
        neg_m[pr] = jnp.where(strict, -a_mat[pr][CHUNK:] * decay, 0.0)
        rhs_wu[pr] = jnp.concatenate([kb * e_cum, v * beta], axis=1).astype(BF16)
        q_dec[pr] = q * e_cum
        k_dec_t[pr] = (k * jnp.exp(cum_last - cum)).T.astype(BF16)
        last[pr] = jnp.exp(cum_last)
    pw = dict(neg_m)
    t_inv = {pr: eye + neg_m[pr] for pr in probs}
    for _ in range(CHUNK.bit_length() - 2):
        for pr in probs:
            p_bf = pw[pr].astype(BF16)
            pw[pr] = _dot(p_bf, p_bf)
        for pr in probs:
            t_inv[pr] = _dot(t_inv[pr].astype(BF16), (eye + pw[pr]).astype(BF16))
    wu = {pr: _dot(t_inv[pr].astype(BF16), rhs_wu[pr]) for pr in probs}

    state = [state_ref[h] for h in range(nh)]
    outs = [[] for _ in range(nh)]
    for c in range(nc):
        ws = [_dot(jnp.concatenate([wu[h, c][:, :d], q_dec[h, c]], axis=0).astype(BF16),
                   state[h].astype(BF16)) for h in range(nh)]
        v_new = [(wu[h, c][:, d:] - ws[h][:CHUNK]).astype(BF16) for h in range(nh)]
        for h in range(nh):
            outs[h].append(ws[h][CHUNK:] + _dot(qk[h, c], v_new[h]))
        state = [state[h] * last[h, c] + _dot(k_dec_t[h, c], v_new[h]) for h in range(nh)]
    for h in range(nh):
        state_ref[h] = state[h]
        o = jnp.concatenate(outs[h], axis=0)
        o = _rms(o, onorm_ref[...]) * _silu(z_ref[:, h * d:(h + 1) * d])
        y_ref[:, h * d:(h + 1) * d] = o.astype(y_ref.dtype)


def _delta_rule(qkv, z, bd, conv_w, avec, dvec, onorm, batch, tb):
    m, c3 = qkv.shape
    s = m // batch
    nsb = s // tb
    cid = np.arange(tb) // CHUNK
    tri = jnp.asarray(((cid[:, None] == cid[None, :]) &
                       (np.arange(tb)[:, None] >= np.arange(tb)[None, :])).astype(np.float32))
    vw = A_HEADS * A_D
    row = lambda b, j: (b * nsb + j, 0)
    return pl.pallas_call(
        functools.partial(_delta_body, tb),
        grid=(batch, nsb),
        in_specs=[pl.BlockSpec((tb, c3), row), pl.BlockSpec((tb, vw), row),
                  pl.BlockSpec((tb, LANES), row), _full(conv_w.shape), _full(avec.shape),
                  _full(dvec.shape), _full(onorm.shape), _full(tri.shape)],
        out_specs=pl.BlockSpec((tb, vw), row),
        out_shape=jax.ShapeDtypeStruct((m, vw), BF16),
        scratch_shapes=[pltpu.VMEM((HALO + tb, c3), F32), pltpu.VMEM((HALO, c3), F32),
                        pltpu.VMEM((A_HEADS, A_D, A_D), F32)],
        compiler_params=_params("parallel", "arbitrary"),
        name="delta_rule",
    )(qkv, z, bd, conv_w, avec, dvec, onorm, tri)


def _gelu_tanh(x):
    return 0.5 * x * (1.0 + jnp.tanh(math.sqrt(2.0 / math.pi) * (x + 0.044715 * (x * x * x))))


def _lru_body(tb, xb_ref, gb_ref, cw_ref, cb_ref, wa_ref, ba_ref, wx_ref, bx_ref, l_ref,
              y_ref, buf_ref, hist_ref, h_ref):
    first = pl.program_id(1) == 0

    @pl.when(first)
    def _():
        h_ref[...] = jnp.zeros_like(h_ref)

    xc = _causal_conv(xb_ref[...], cw_ref, buf_ref, hist_ref, first) + cb_ref[...]
    xc_bf = xc.astype(BF16)
    r = _sigmoid(_dot(xc_bf, wa_ref[...]) + ba_ref[...])
    i = _sigmoid(_dot(xc_bf, wx_ref[...]) + bx_ref[...])
    log_a = (-LRU_C) * r * _softplus(-l_ref[...])
    a = jnp.exp(log_a)
    th = jnp.tanh(log_a)
    u = jnp.sqrt(-2.0 * th / (1.0 - th)) * (i * xc)
    rows = lax.broadcasted_iota(jnp.int32, a.shape, 0)
    shift = 1
    while shift < tb:
        keep = rows >= shift
        a_prev = jnp.where(keep, pltpu.roll(a, shift, 0), 1.0)
        u_prev = jnp.where(keep, pltpu.roll(u, shift, 0), 0.0)
        u = u + a * u_prev
        a = a * a_prev
        shift *= 2
    hs = u + a * h_ref[...]
    h_ref[...] = hs[tb - 1:tb, :]
    y_ref[...] = (_gelu_tanh(gb_ref[...]) * hs).astype(y_ref.dtype)


def _rglru(xb, gb, cw, cb, wa, ba, wx, bx, lru_l, batch, tb):
    m, wdt = xb.shape
    nsb = (m // batch) // tb
    row = lambda b, j: (b * nsb + j, 0)
    return pl.pallas_call(
        functools.partial(_lru_body, tb),
        grid=(batch, nsb),
        in_specs=[pl.BlockSpec((tb, wdt), row), pl.BlockSpec((tb, wdt), row),
                  _full(cw.shape), _full(cb.shape), _full(wa.shape), _full(ba.shape),
                  _full(wx.shape), _full(bx.shape), _full(lru_l.shape)],
        out_specs=pl.BlockSpec((tb, wdt), row),
        out_shape=jax.ShapeDtypeStruct((m, wdt), BF16),
        scratch_shapes=[pltpu.VMEM((HALO + tb, wdt), F32), pltpu.VMEM((HALO, wdt), F32),
                        pltpu.VMEM((1, wdt), F32)],
        compiler_params=_params("parallel", "arbitrary"),
        name="rglru",
    )(xb, gb, cw, cb, wa, ba, wx, bx, lru_l)


def _head_norm(x, nblk, denom):
    outs = []
    for h in range(nblk):
        xh = x[:, h * LANES:(h + 1) * LANES]
        ss = jnp.sum(xh * xh, axis=-1, keepdims=True)
        outs.append(xh * lax.rsqrt(ss * (1.0 / denom) + NORM_EPS))
    return jnp.concatenate(outs, axis=1)


def _half_norm(x, nblk):
    lane = lax.broadcasted_iota(jnp.int32, (1, LANES), 1)
    lo = lane < C_DK
    outs = []
    for h in range(nblk):
        xh = x[:, h * LANES:(h + 1) * LANES]
        sq = xh * xh
        s_lo = jnp.sum(jnp.where(lo, sq, 0.0), axis=-1, keepdims=True)
        s_hi = jnp.sum(jnp.where(lo, 0.0, sq), axis=-1, keepdims=True)
        inv = jnp.where(lo, lax.rsqrt(s_lo * (1.0 / C_DK) + NORM_EPS),
                        lax.rsqrt(s_hi * (1.0 / C_DK) + NORM_EPS))
        outs.append(xh * inv)
    return jnp.concatenate(outs, axis=1)


def _rope_blocks(x, nblk, cos, sin):
    lane = lax.broadcasted_iota(jnp.int32, (1, LANES), 1)
    first_half = lane < D_NOPE + D_ROPE // 2
    outs = []
    for h in range(nblk):
        xh = x[:, h * LANES:(h + 1) * LANES]
        partner = jnp.where(first_half, pltpu.roll(xh, LANES - D_ROPE // 2, 1),
                            pltpu.roll(xh, D_ROPE // 2, 1))
        outs.append(xh * cos + partner * sin)
    return jnp.concatenate(outs, axis=1)


def _store_vt(vt_ref, v, t):
    vt = v.T
    for j in range(v.shape[0] // t):
        vt_ref[j] = vt[:, j * t:(j + 1) * t].astype(vt_ref.dtype)


def _odd_in_body(tks, x_ref, pos_ref, g_ref, w_ref, cqn_ref, ckn_ref, qln_ref, wuq_ref, kvn_ref,
                 wukv_ref, dqn_ref, dkn_ref, freq_ref,
                 qc_ref, kc_ref, vc_ref, qd_ref, kd_ref, vd_ref):
    hw = C_HEADS * LANES
    h = _rms(x_ref[...], g_ref[...]).astype(BF16)
    qc = _dot(h, w_ref[:, 0:hw])
    qc_ref[...] = (_half_norm(qc, C_HEADS) * (cqn_ref[...] * (C_DK ** -0.5 * LOG2E))).astype(BF16)
    kc = _dot(h, w_ref[:, hw:2 * hw])
    kc_ref[...] = (_half_norm(kc, C_HEADS) * ckn_ref[...]).astype(BF16)
    _store_vt(vc_ref, _dot(h, w_ref[:, 2 * hw:3 * hw]), tks[0])

    lane = lax.broadcasted_iota(jnp.int32, (1, LANES), 1)
    ang = pos_ref[...].astype(F32) * freq_ref[...]
    is_rope = (lane >= D_NOPE) & (lane < D_QK)
    cos = jnp.where(is_rope, jnp.cos(ang), 1.0)
    sin_raw = jnp.sin(ang)
    sin = jnp.where(is_rope, jnp.where(lane < D_NOPE + D_ROPE // 2, -sin_raw, sin_raw), 0.0)

    off = 3 * hw
    q_lat = _dot(h, w_ref[:, off:off + Q_LORA])
    qd = _dot(_rms(q_lat, qln_ref[...]).astype(BF16), wuq_ref[...])
    qd = _head_norm(qd, D_HEADS, D_QK) * (dqn_ref[...] * (D_QK ** -0.5 * LOG2E))
    qd_ref[...] = _rope_blocks(qd, D_HEADS, cos, sin).astype(BF16)

    off += Q_LORA
    kv_lat = _dot(h, w_ref[:, off:off + KV_LORA])
    k_rope = _dot(h, w_ref[:, off + KV_LORA:off + KV_LORA + LANES])
    kvd = _dot(_rms(kv_lat, kvn_ref[...]).astype(BF16), wukv_ref[...])
    dhw = D_HEADS * LANES
    kd = kvd[:, :dhw] + jnp.concatenate([k_rope] * D_HEADS, axis=1)
    kd = _head_norm(kd, D_HEADS, D_QK) * dkn_ref[...]
    kd_ref[...] = _rope_blocks(kd, D_HEADS, cos, sin).astype(BF16)
    _store_vt(vd_ref, kvd[:, dhw:], tks[1])


def _odd_in(x, pos, g, w, cqn, ckn, qln, wuq, kvn, wukv, dqn, dkn, freq, tm, tks):
    m, d = x.shape
    hw = C_HEADS * LANES
    row = lambda i: (i, 0)
    small = [g, w, cqn, ckn, qln, wuq, kvn, wukv, dqn, dkn, freq]
    rspec = pl.BlockSpec((tm, hw), row)
    tspec = [pl.BlockSpec((tm // t, hw, t), lambda i: (i, 0, 0)) for t in tks]
    rshape = jax.ShapeDtypeStruct((m, hw), BF16)
    tshape = [jax.ShapeDtypeStruct((m // t, hw, t), BF16) for t in tks]
    return pl.pallas_call(
        functools.partial(_odd_in_body, tks),
        grid=(m // tm,),
        in_specs=[pl.BlockSpec((tm, d), row), pl.BlockSpec((tm, 1), row)] + [_full(a.shape) for a in small],
        out_specs=[rspec, rspec, tspec[0], rspec, rspec, tspec[1]],
        out_shape=[rshape, rshape, tshape[0], rshape, rshape, tshape[1]],
        compiler_params=_params("parallel"),
        name="odd_in",
    )(x, pos, *small)


LOG2E = math.log2(math.e)
TK_DIFF = 256
TK_MLA = 512


def _flash_body(n_maps, alibi, t, lam_init, *refs):
    if alibi:
        lam_ref, sub_ref, q_ref, k_ref, vt_ref, pq_ref, pk_ref, o_ref = refs
    else:
        q_ref, k_ref, vt_ref, o_ref = refs
    nq = q_ref.shape[0] // t
    nh = q_ref.shape[1] // LANES
    tk = vt_ref.shape[2]
    lane = lax.broadcasted_iota(jnp.int32, (1, LANES), 1)
    key_i = lax.broadcasted_iota(jnp.int32, (tk, t), 0)
    qry_i = lax.broadcasted_iota(jnp.int32, (tk, t), 1)
    shift = CHUNK.bit_length() - 1
    chains = [(h, mp) for h in range(nh) for mp in range(n_maps)]

    def q_block(qi, _):
        rows = pl.ds(pl.multiple_of(qi * t, t), t)
        qm = {}
        for h in range(nh):
            q = q_ref[rows, h * LANES:(h + 1) * LANES]
            for mp in range(n_maps):
                qm[h, mp] = (jnp.where((lane < C_DK) == (mp == 0), q, jnp.zeros_like(q))
                             if n_maps == 2 else q)
        if alibi:
            pq = pq_ref[0, qi].astype(F32)

        def group(g, carry, masked):
            krows = pl.ds(pl.multiple_of(g * tk, tk), tk)
            s = {ch: _dot_nt(k_ref[krows, ch[0] * LANES:(ch[0] + 1) * LANES], qm[ch]) for ch in chains}
            if alibi:
                dist = jnp.abs(pk_ref[krows, :].astype(F32) - pq)
                for ch in chains:
                    s[ch] = s[ch] - (2.0 ** (-2 * (ch[0] + 1)) * LOG2E) * dist
            if masked:
                allowed = ((key_i + g * tk) >> shift) <= ((qry_i + qi * t) >> shift)
                for ch in chains:
                    s[ch] = jnp.where(allowed, s[ch], NEG_INF)
            vt = vt_ref[g]
            new = []
            for ch, (m_i, l_i, acc) in zip(chains, carry):
                m_new = jnp.maximum(m_i, jnp.max(s[ch], axis=0, keepdims=True))
                alpha = jnp.exp2(m_i - m_new)
                p = jnp.exp2(s[ch] - m_new)
                l_new = alpha * l_i + jnp.sum(p, axis=0, keepdims=True)
                pv = _dot(vt[ch[0] * LANES:(ch[0] + 1) * LANES, :], p.astype(BF16))
                new.append((m_new, l_new, alpha * acc + pv))
            return tuple(new)

        init = tuple((jnp.full((1, t), NEG_INF, F32), jnp.zeros((1, t), F32), jnp.zeros((LANES, t), F32))
                     for _ in chains)
        g_last = (qi * t) // tk
        carry = lax.fori_loop(0, g_last, lambda g, c: group(g, c, False), init)
        carry = group(g_last, carry, True)
        outs = {ch: acc / l_i for ch, (_, l_i, acc) in zip(chains, carry)}
        if n_maps == 2:
            lv = lam_ref[...]
            lam = (jnp.exp(jnp.sum(lv[0:1] * lv[1:2], axis=-1, keepdims=True))
                   - jnp.exp(jnp.sum(lv[2:3] * lv[3:4], axis=-1, keepdims=True)) + lam_init)
        for h in range(nh):
            if n_maps == 2:
                o = (outs[h, 0] - lam * outs[h, 1]).T
                o = _rms(o, sub_ref[...]) * (1.0 - lam_init)
            else:
                o = outs[h, 0].T
            o_ref[rows, h * LANES:(h + 1) * LANES] = o.astype(o_ref.dtype)
        return 0

    lax.fori_loop(0, nq, q_block, 0)


def _flash(q, k, vt, batch, t, n_maps, extras=None, lam_init=0.0):
    m, hw = q.shape
    s = m // batch
    nq = s // t
    tk = vt.shape[2]
    alibi = extras is not None
    spec = pl.BlockSpec((s, hw), lambda b: (b, 0))
    in_specs = [spec, spec, pl.BlockSpec((s // tk, hw, tk), lambda b: (b, 0, 0))]
    args = [q, k, vt]
    if alibi:
        lamv, subn, pos_col, pos_row = extras
        in_specs = [_full(lamv.shape), _full(subn.shape)] + in_specs + [
            pl.BlockSpec((1, nq, 1, t), lambda b: (b, 0, 0, 0)),
            pl.BlockSpec((s, 1), lambda b: (b, 0))]
        args = [lamv, subn] + args + [pos_row, pos_col]
    return pl.pallas_call(
        functools.partial(_flash_body, n_maps, alibi, t, lam_init),
        grid=(batch,),
        in_specs=in_specs,
        out_specs=spec,
        out_shape=jax.ShapeDtypeStruct((m, hw), BF16),
        compiler_params=_params("parallel"),
        name="flash_diff" if alibi else "flash_mla",
    )(*args)


def _mem_kv_body(mem_ref, g_ref, w_ref, kn_ref, k_ref, v_ref):
    hw = X_HEADS * X_DH
    mn = _rms(mem_ref[...], g_ref[...]).astype(BF16)
    k = _dot(mn, w_ref[:, :hw])
    k_ref[...] = (_head_norm(k, X_HEADS, X_DH) * kn_ref[...]).astype(BF16)
    v_ref[...] = _dot(mn, w_ref[:, hw:]).astype(BF16)


def _mem_kv(mem2, g, w, kn, batch):
    rows, d = mem2.shape
    nm = rows // batch
    hw = X_HEADS * X_DH
    return pl.pallas_call(
        _mem_kv_body,
        grid=(batch,),
        in_specs=[pl.BlockSpec((nm, d), lambda b: (b, 0)), _full(g.shape), _full(w.shape), _full(kn.shape)],
        out_specs=[pl.BlockSpec((nm, hw), lambda b: (b, 0))] * 2,
        out_shape=[jax.ShapeDtypeStruct((rows, hw), BF16)] * 2,
        compiler_params=_params("parallel"),
        name="mem_kv",
    )(mem2, g, w, kn)


def _cross_body(x_ref, g_ref, wq_ref, qn_ref, k_ref, v_ref, wo_ref, o_ref):
    x = x_ref[...]
    q = _dot(_rms(x, g_ref[...]).astype(BF16), wq_ref[...])
    q = (_head_norm(q, X_HEADS, X_DH) * (qn_ref[...] * X_DH ** -0.5)).astype(BF16)
    outs = []
    for h in range(X_HEADS):
        blk = slice(h * X_DH, (h + 1) * X_DH)
        s = _dot_nt(q[:, blk], k_ref[:, blk])
        p = jnp.exp(s - jnp.max(s, axis=-1, keepdims=True))
        l = jnp.sum(p, axis=-1, keepdims=True)
        outs.append((_dot(p.astype(BF16), v_ref[:, blk]) / l).astype(BF16))
    o_ref[...] = x + _dot(jnp.concatenate(outs, axis=1), wo_ref[...])


def _cross_attn(x, g, wq, qn, k, v, wo, batch, tm):
    m, d = x.shape
    per = (m // batch) // tm
    nm = k.shape[0] // batch
    hw = X_HEADS * X_DH
    kv = pl.BlockSpec((nm, hw), lambda i: (i // per, 0))
    return pl.pallas_call(
        _cross_body,
        grid=(m // tm,),
        in_specs=[pl.BlockSpec((tm, d), lambda i: (i, 0)), _full(g.shape), _full(wq.shape),
                  _full(qn.shape), kv, kv, _full(wo.shape)],
        out_specs=pl.BlockSpec((tm, d), lambda i: (i, 0)),
        out_shape=jax.ShapeDtypeStruct((m, d), F32),
        compiler_params=_params("parallel"),
        name="cross_attn",
    )(x, g, wq, qn, k, v, wo)


def _ffn_body(hid, th, x_ref, g_ref, wi_ref, wo_ref, o_ref):
    x = x_ref[...]
    h = _rms(x, g_ref[...]).astype(BF16)
    acc = x
    for c in range(hid // th):
        gate = _dot(h, wi_ref[:, c * th:(c + 1) * th])
        up = _dot(h, wi_ref[:, hid + c * th:hid + (c + 1) * th])
        acc = acc + _dot((_silu(gate) * up).astype(BF16), wo_ref[c * th:(c + 1) * th, :])
    o_ref[...] = acc


def _ffn(x, g, wi, wo, tm, th):
    m, d = x.shape
    hid = wo.shape[0]
    return pl.pallas_call(
        functools.partial(_ffn_body, hid, th),
        grid=(m // tm,),
        in_specs=[pl.BlockSpec((tm, d), lambda i: (i, 0)), _full(g.shape),
                  pl.BlockSpec(wi.shape, lambda i: (0, 0), pipeline_mode=pl.Buffered(1)),
                  pl.BlockSpec(wo.shape, lambda i: (0, 0), pipeline_mode=pl.Buffered(1))],
        out_specs=pl.BlockSpec((tm, d), lambda i: (i, 0)),
        out_shape=jax.ShapeDtypeStruct((m, d), F32),
        compiler_params=_params("parallel"),
        name="ffn",
    )(x, g, wi, wo)


def _row(v):
    return v.reshape(1, -1).astype(F32)


def _pad_cols(w, width):
    return jnp.pad(w, ((0, 0), (0, width - w.shape[1])))


def _block_diag(w):
    n, d, e = w.shape
    eye = jnp.eye(n, dtype=w.dtype)
    return (eye[:, None, :, None] * w[:, :, None, :]).reshape(n * d, n * e)


def _even_layer(x, batch, tm, tb, norm_g, w_in, conv_qkv, a_log, dt_bias, o_norm, conv_b_w, conv_b_b,
                gate_a_w, gate_a_b, gate_x_w, gate_x_b, lru_l, w_out):
    qk = A_HEADS * A_D
    c_qkv, c_z = 3 * qk, qk
    o = 0
    w_qkv = w_in[:, o:o + c_qkv]; o += c_qkv
    w_z = w_in[:, o:o + c_z]; o += c_z
    w_bd = w_in[:, o:o + 2 * A_HEADS]; o += 2 * A_HEADS
    w_xb = w_in[:, o:o + B_WIDTH]; o += B_WIDTH
    w_gb = w_in[:, o:o + B_WIDTH]
    w = jnp.concatenate([w_qkv, w_z, w_xb, w_gb, _pad_cols(w_bd, LANES)], axis=1).astype(BF16)
    qkv, z, xb, gb, bd = _norm_proj(x, _row(norm_g), w, [c_qkv, c_z, B_WIDTH, B_WIDTH, LANES],
                                    [F32] * 5, tm)
    pad = jnp.zeros((A_HEADS,), F32)
    avec = _row(jnp.pad(jnp.concatenate([pad, a_log.astype(F32)]), (0, LANES - 2 * A_HEADS)))
    dvec = _row(jnp.pad(jnp.concatenate([pad, dt_bias.astype(F32)]), (0, LANES - 2 * A_HEADS)))
    y_a = _delta_rule(qkv, z, bd, conv_qkv.astype(F32), avec, dvec, _row(o_norm), batch, tb)
    y_b = _rglru(xb, gb, conv_b_w.astype(F32), _row(conv_b_b), _block_diag(gate_a_w).astype(BF16),
                 _row(gate_a_b), _block_diag(gate_x_w).astype(BF16), _row(gate_x_b), _row(lru_l),
                 batch, tb)
    return _out_proj(x, y_a, y_b, w_out.astype(BF16), tm)


def _head_pad_cols(w, heads, real, total):
    k = w.shape[0]
    return jnp.pad(w.reshape(k, heads, real), ((0, 0), (0, 0), (0, total - real))).reshape(k, heads * total)


def _odd_layer(x, pos_col, pos_row, batch, tm, tq, layer_idx, norm_g, w_in, c_q_norm, c_k_norm,
               lam_q1, lam_k1, lam_q2, lam_k2, c_sub_norm, q_lat_norm, w_uq, kv_lat_norm, w_ukv,
               d_q_norm, d_k_norm, w_out):
    hw = C_HEADS * 2 * C_DK
    k_rope_w = jnp.pad(w_in[:, 3 * hw + Q_LORA + KV_LORA:], ((0, 0), (D_NOPE, LANES - D_QK)))
    w = jnp.concatenate([w_in[:, :3 * hw + Q_LORA + KV_LORA], k_rope_w], axis=1).astype(BF16)
    wuq = _head_pad_cols(w_uq, D_HEADS, D_QK, LANES).astype(BF16)
    kvr = w_ukv.reshape(KV_LORA, D_HEADS, D_NOPE + LANES)
    wukv = jnp.concatenate([_head_pad_cols(kvr[:, :, :D_NOPE].reshape(KV_LORA, -1), D_HEADS, D_NOPE, LANES),
                            kvr[:, :, D_NOPE:].reshape(KV_LORA, -1)], axis=1).astype(BF16)
    tile = lambda v, n: _row(jnp.tile(v.astype(F32), n))
    padn = lambda v: jnp.pad(v.astype(F32), (0, LANES - D_QK))
    half = D_ROPE // 2
    inv_freq = ROPE_THETA ** (-jnp.arange(half, dtype=F32) / half)
    freq = _row(jnp.concatenate([jnp.zeros((D_NOPE,), F32), inv_freq, inv_freq,
                                 jnp.zeros((LANES - D_QK,), F32)]))
    qc, kc, vc, qd, kd, vd = _odd_in(
        x, pos_col, _row(norm_g), w, tile(c_q_norm, 2 * C_HEADS), tile(c_k_norm, 2 * C_HEADS),
        _row(q_lat_norm), wuq, _row(kv_lat_norm), wukv, tile(padn(d_q_norm), D_HEADS),
        tile(padn(d_k_norm), D_HEADS), freq, tm, (min(TK_DIFF, tm), min(TK_MLA, tm)))
    lam_init = 0.8 - 0.6 * math.exp(-0.3 * layer_idx)
    lamv = jnp.stack([lam_q1, lam_k1, lam_q2, lam_k2]).astype(F32)
    yc = _flash(qc, kc, vc, batch, tq, 2, (lamv, _row(c_sub_norm), pos_col, pos_row), lam_init)
    yd = _flash(qd, kd, vd, batch, tq, 1)
    return _out_proj(x, yc, yd, w_out.astype(BF16), tm)


def kernel(x, mem, positions, norm_mix, norm_x, norm_mem, x_wq, x_wkv, x_q_norm, x_k_norm, x_wo, norm_ffn, ffn_w_in, ffn_w_out, ev_w_in, ev_conv_qkv, ev_a_log, ev_dt_bias, ev_o_norm, ev_conv_b_w, ev_conv_b_b, ev_gate_a_w, ev_gate_a_b, ev_gate_x_w, ev_gate_x_b, ev_lru_l, ev_w_out, od_w_in, od_c_q_norm, od_c_k_norm, od_lam_q1, od_lam_k1, od_lam_q2, od_lam_k2, od_c_sub_norm, od_q_lat_norm, od_w_uq, od_kv_lat_norm, od_w_ukv, od_d_q_norm, od_d_k_norm, od_w_out):
    batch, seq, d = x.shape
    depth = norm_mix.shape[0]
    m = batch * seq
    tm = min(512, seq)
    tb = min(256, seq)
    tq = min(LANES, seq)
    th = ffn_w_out.shape[1]
    th = 1408 if th % 1408 == 0 else th
    xf = x.reshape(m, d).astype(F32)
    mem2 = mem.reshape(-1, d).astype(F32)
    pos_col = positions.reshape(m, 1).astype(jnp.int32)
    pos_row = positions.reshape(batch, seq // tq, 1, tq).astype(jnp.int32)
    for l in range(depth):
        i = l // 2
        if l % 2 == 0:
            xf = _even_layer(xf, batch, tm, tb, norm_mix[l], ev_w_in[i], ev_conv_qkv[i], ev_a_log[i],
                             ev_dt_bias[i], ev_o_norm[i], ev_conv_b_w[i], ev_conv_b_b[i], ev_gate_a_w[i],
                             ev_gate_a_b[i], ev_gate_x_w[i], ev_gate_x_b[i], ev_lru_l[i], ev_w_out[i])
        else:
            xf = _odd_layer(xf, pos_col, pos_row, batch, tm, tq, l, norm_mix[l], od_w_in[i],
                            od_c_q_norm[i], od_c_k_norm[i], od_lam_q1[i], od_lam_k1[i], od_lam_q2[i],
                            od_lam_k2[i], od_c_sub_norm[i], od_q_lat_norm[i], od_w_uq[i],
                            od_kv_lat_norm[i], od_w_ukv[i], od_d_q_norm[i], od_d_k_norm[i], od_w_out[i])
        k_mem, v_mem = _mem_kv(mem2, _row(norm_mem[l]), x_wkv[l].astype(BF16),
                               _row(jnp.tile(x_k_norm[l], X_HEADS)), batch)
        xf = _cross_attn(xf, _row(norm_x[l]), x_wq[l].astype(BF16), _row(jnp.tile(x_q_norm[l], X_HEADS)),
                         k_mem, v_mem, x_wo[l].astype(BF16), batch, tm)
        xf = _ffn(xf, _row(norm_ffn[l]), ffn_w_in[l].astype(BF16), ffn_w_out[l].astype(BF16), tm, th)
    return xf.reshape(batch, seq, d).astype(x.dtype)
```

```python
import functools
import math

import jax
import jax.numpy as jnp
import numpy as np
from jax import lax
from jax.experimental import pallas as pl
from jax.experimental.pallas import tpu as pltpu

F32 = jnp.float32
BF16 = jnp.bfloat16

NORM_EPS = 1e-6
NEG_INF = -1e30
CHUNK = 64
LANES = 128
LRU_C = 8.0
ROPE_THETA = 10000.0
VMEM_LIMIT = 56 * 1024 * 1024

A_HEADS = 4
A_D = 128
B_WIDTH = 512
B_BLOCKS = 8
C_HEADS = 4
C_DK = 64
D_HEADS = 4
D_NOPE = 64
D_ROPE = 32
D_QK = D_NOPE + D_ROPE
Q_LORA = 256
KV_LORA = 128
X_HEADS = 4
X_DH = 128


def _params(*sem):
    return pltpu.CompilerParams(dimension_semantics=sem, vmem_limit_bytes=VMEM_LIMIT)


def _rms(x, g):
    ms = jnp.mean(x * x, axis=-1, keepdims=True)
    return x * lax.rsqrt(ms + NORM_EPS) * g


def _dot(a, b):
    return jnp.dot(a, b, preferred_element_type=F32)


def _dot_nt(a, b):
    return lax.dot_general(a, b, (((1,), (1,)), ((), ())), preferred_element_type=F32)


def _sigmoid(x):
    return 1.0 / (1.0 + jnp.exp(-x))


def _silu(x):
    return x * _sigmoid(x)


def _softplus(x):
    return jnp.maximum(x, 0.0) + jnp.log(1.0 + jnp.exp(-jnp.abs(x)))


def _full(shape):
    return pl.BlockSpec(shape, lambda *_: (0,) * len(shape))


def _norm_proj_body(splits, x_ref, g_ref, w_ref, *out_refs):
    h = _rms(x_ref[...], g_ref[...]).astype(BF16)
    for (start, width), o_ref in zip(splits, out_refs):
        o_ref[...] = _dot(h, w_ref[:, start:start + width]).astype(o_ref.dtype)


def _norm_proj(x, g, w, widths, dtypes, tm):
    m, d = x.shape
    starts = np.concatenate([[0], np.cumsum(widths)[:-1]]).tolist()
    splits = tuple(zip(starts, widths))
    return pl.pallas_call(
        functools.partial(_norm_proj_body, splits),
        grid=(m // tm,),
        in_specs=[pl.BlockSpec((tm, d), lambda i: (i, 0)), _full(g.shape), _full(w.shape)],
        out_specs=[pl.BlockSpec((tm, wd), lambda i: (i, 0)) for wd in widths],
        out_shape=[jax.ShapeDtypeStruct((m, wd), dt) for wd, dt in zip(widths, dtypes)],
        compiler_params=_params("parallel"),
        name="norm_proj",
    )(x, g, w)


HALO = 8


def _causal_conv(x, w_ref, buf_ref, hist_ref, first):
    tb = x.shape[0]
    k = w_ref.shape[0]

    @pl.when(first)
    def _():
        hist_ref[...] = jnp.zeros_like(hist_ref)

    buf_ref[0:HALO, :] = hist_ref[...]
    buf_ref[HALO:, :] = x
    hist_ref[...] = x[tb - HALO:, :]
    y = buf_ref[pl.ds(HALO - (k - 1), tb), :] * w_ref[0:1, :]
    for j in range(1, k):
        y = y + buf_ref[pl.ds(HALO - (k - 1) + j, tb), :] * w_ref[j:j + 1, :]
    return y


def _delta_body(tb, qkv_ref, z_ref, bd_ref, cw_ref, avec_ref, dvec_ref, onorm_ref, tri_ref,
                y_ref, buf_ref, hist_ref, state_ref):
    first = pl.program_id(1) == 0
    nh, d = A_HEADS, A_D
    qk_w = nh * d

    @pl.when(first)
    def _():
        state_ref[...] = jnp.zeros_like(state_ref)

    act = _silu(_causal_conv(qkv_ref[...], cw_ref, buf_ref, hist_ref, first))

    bd = bd_ref[...]
    beta_all = _sigmoid(bd)
    g_all = -jnp.exp(avec_ref[...]) * _softplus(bd + dvec_ref[...])
    cum_all = jnp.dot(tri_ref[...], g_all, precision=lax.Precision.HIGHEST,
                      preferred_element_type=F32)

    cum_t = cum_all.T

    ri = lax.broadcasted_iota(jnp.int32, (CHUNK, CHUNK), 0)
    ci = lax.broadcasted_iota(jnp.int32, (CHUNK, CHUNK), 1)
    incl = ri >= ci
    strict = ri > ci
    eye = jnp.where(ri == ci, 1.0, 0.0).astype(F32)
    scale = d ** -0.5

    nc = tb // CHUNK
    probs = [(h, c) for c in range(nc) for h in range(nh)]
    qs, ks, vs = {}, {}, {}
    for h in range(nh):
        q_h = act[:, h * d:(h + 1) * d]
        k_h = act[:, qk_w + h * d:qk_w + (h + 1) * d]
        q_h = q_h * (lax.rsqrt(jnp.sum(q_h * q_h, axis=-1, keepdims=True) + NORM_EPS) * scale)
        k_h = k_h * lax.rsqrt(jnp.sum(k_h * k_h, axis=-1, keepdims=True) + NORM_EPS)
        for c in range(nc):
            rows = slice(c * CHUNK, (c + 1) * CHUNK)
            qs[h, c] = q_h[rows]
            ks[h, c] = k_h[rows]
            vs[h, c] = act[rows, 2 * qk_w + h * d:2 * qk_w + (h + 1) * d]

    a_mat, rhs_wu, q_dec, k_dec_t, qk, neg_m, last = {}, {}, {}, {}, {}, {}, {}
    for pr in probs:
        h, c = pr
        rows = slice(c * CHUNK, (c + 1) * CHUNK)
        q, k, v = qs[pr], ks[pr], vs[pr]
        beta = beta_all[rows, h:h + 1]
        cum = cum_all[rows, nh + h:nh + h + 1]
        cum_row = cum_t[nh + h:nh + h + 1, c * CHUNK:(c + 1) * CHUNK]
        cum_last = cum[CHUNK - 1:CHUNK, :]
        decay = jnp.where(incl, jnp.exp(jnp.where(incl, cum - cum_row, 0.0)), 0.0)
        e_cum = jnp.exp(cum)
        kb = k * beta
        a_mat[pr] = _dot_nt(jnp.concatenate([q, kb], axis=0).astype(BF16), k.astype(BF16))
        qk[pr] = jnp.where(incl, a_mat[pr][:CHUNK] * decay, 0.0).astype(BF16)
        neg_m[pr] = jnp.where(strict, -a_mat[pr][CHUNK:] * decay, 0.0)
        rhs_wu[pr] = jnp.concatenate([kb * e_cum, v * beta], axis=1).astype(BF16)
        q_dec[pr] = q * e_cum
        k_dec_t[pr] = (k * jnp.exp(cum_last - cum)).T.astype(BF16)
        last[pr] = jnp.exp(cum_last)
    pw = dict(neg_m)
    t_inv = {pr: eye + neg_m[pr] for pr in probs}
    for _ in range(CHUNK.bit_length() - 2):
        for pr in probs:
            p_bf = pw[pr].astype(BF16)
            pw[pr] = _dot(p_bf, p_bf)
        for pr in probs:
            t_inv[pr] = _dot(t_inv[pr].astype(BF16), (eye + pw[pr]).astype(BF16))
    wu = {pr: _dot(t_inv[pr].astype(BF16), rhs_wu[pr]) for pr in probs}

    state = [state_ref[h] for h in range(nh)]
    outs = [[] for _ in range(nh)]
    for c in range(nc):
        ws = [_dot(jnp.concatenate([wu[h, c][:, :d], q_dec[h, c]], axis=0).astype(BF16),
                   state[h].astype(BF16)) for h in range(nh)]
        v_new = [(wu[h, c][:, d:] - ws[h][:CHUNK]).astype(BF16) for h in range(nh)]
        for h in range(nh):
            outs[h].append(ws[h][CHUNK:] + _dot(qk[h, c], v_new[h]))
        state = [state[h] * last[h, c] + _dot(k_dec_t[h, c], v_new[h]) for h in range(nh)]
    for h in range(nh):
        state_ref[h] = state[h]
        o = jnp.concatenate(outs[h], axis=0)
        o = _rms(o, onorm_ref[...]) * _silu(z_ref[:, h * d:(h + 1) * d])
        y_ref[:, h * d:(h + 1) * d] = o.astype(y_ref.dtype)


def _delta_rule(qkv, z, bd, conv_w, avec, dvec, onorm, batch, tb):
    m, c3 = qkv.shape
    s = m // batch
    nsb = s // tb
    cid = np.arange(tb) // CHUNK
    tri = jnp.asarray(((cid[:, None] == cid[None, :]) &
                       (np.arange(tb)[:, None] >= np.arange(tb)[None, :])).astype(np.float32))
    vw = A_HEADS * A_D
    row = lambda b, j: (b * nsb + j, 0)
    return pl.pallas_call(
        functools.partial(_delta_body, tb),
        grid=(batch, nsb),
        in_specs=[pl.BlockSpec((tb, c3), row), pl.BlockSpec((tb, vw), row),
                  pl.BlockSpec((tb, LANES), row), _full(conv_w.shape), _full(avec.shape),
                  _full(dvec.shape), _full(onorm.shape), _full(tri.shape)],
        out_specs=pl.BlockSpec((tb, vw), row),
        out_shape=jax.ShapeDtypeStruct((m, vw), BF16),
        scratch_shapes=[pltpu.VMEM((HALO + tb, c3), F32), pltpu.VMEM((HALO, c3), F32),
                        pltpu.VMEM((A_HEADS, A_D, A_D), F32)],
        compiler_params=_params("parallel", "arbitrary"),
        name="delta_rule",
    )(qkv, z, bd, conv_w, avec, dvec, onorm, tri)


def _gelu_tanh(x):
    return 0.5 * x * (1.0 + jnp.tanh(math.sqrt(2.0 / math.pi) * (x + 0.044715 * (x * x * x))))


def _lru_body(tb, xb_ref, gb_ref, cw_ref, cb_ref, wa_ref, ba_ref, wx_ref, bx_ref, l_ref,
              y_ref, buf_ref, hist_ref, h_ref):
    first = pl.program_id(1) == 0

    @pl.when(first)
    def _():
        h_ref[...] = jnp.zeros_like(h_ref)

    xc = _causal_conv(xb_ref[...], cw_ref, buf_ref, hist_ref, first) + cb_ref[...]
    xc_bf = xc.astype(BF16)
    r = _sigmoid(_dot(xc_bf, wa_ref[...]) + ba_ref[...])
    i = _sigmoid(_dot(xc_bf, wx_ref[...]) + bx_ref[...])
    log_a = (-LRU_C) * r * _softplus(-l_ref[...])
    a = jnp.exp(log_a)
    th = jnp.tanh(log_a)
    u = jnp.sqrt(-2.0 * th / (1.0 - th)) * (i * xc)
    rows = lax.broadcasted_iota(jnp.int32, a.shape, 0)
    shift = 1
    while shift < tb:
        keep = rows >= shift
        a_prev = jnp.where(keep, pltpu.roll(a, shift, 0), 1.0)
        u_prev = jnp.where(keep, pltpu.roll(u, shift, 0), 0.0)
        u = u + a * u_prev
        a = a * a_prev
        shift *= 2
    hs = u + a * h_ref[...]
    h_ref[...] = hs[tb - 1:tb, :]
    y_ref[...] = (_gelu_tanh(gb_ref[...]) * hs).astype(y_ref.dtype)


def _rglru(xb, gb, cw, cb, wa, ba, wx, bx, lru_l, batch, tb):
    m, wdt = xb.shape
    nsb = (m // batch) // tb
    row = lambda b, j: (b * nsb + j, 0)
    return pl.pallas_call(
        functools.partial(_lru_body, tb),
        grid=(batch, nsb),
        in_specs=[pl.BlockSpec((tb, wdt), row), pl.BlockSpec((tb, wdt), row),
                  _full(cw.shape), _full(cb.shape), _full(wa.shape), _full(ba.shape),
                  _full(wx.shape), _full(bx.shape), _full(lru_l.shape)],
        out_specs=pl.BlockSpec((tb, wdt), row),
        out_shape=jax.ShapeDtypeStruct((m, wdt), BF16),
        scratch_shapes=[pltpu.VMEM((HALO + tb, wdt), F32), pltpu.VMEM((HALO, wdt), F32),
                        pltpu.VMEM((1, wdt), F32)],
        compiler_params=_params("parallel", "arbitrary"),
        name="rglru",
    )(xb, gb, cw, cb, wa, ba, wx, bx, lru_l)


def _head_norm(x, nblk, denom):
    outs = []
    for h in range(nblk):
        xh = x[:, h * LANES:(h + 1) * LANES]
        ss = jnp.sum(xh * xh, axis=-1, keepdims=True)
        outs.append(xh * lax.rsqrt(ss * (1.0 / denom) + NORM_EPS))
    return jnp.concatenate(outs, axis=1)


def _half_norm(x, nblk):
    lane = lax.broadcasted_iota(jnp.int32, (1, LANES), 1)
    lo = lane < C_DK
    outs = []
    for h in range(nblk):
        xh = x[:, h * LANES:(h + 1) * LANES]
        sq = xh * xh
        s_lo = jnp.sum(jnp.where(lo, sq, 0.0), axis=-1, keepdims=True)
        s_hi = jnp.sum(jnp.where(lo, 0.0, sq), axis=-1, keepdims=True)
        inv = jnp.where(lo, lax.rsqrt(s_lo * (1.0 / C_DK) + NORM_EPS),
                        lax.rsqrt(s_hi * (1.0 / C_DK) + NORM_EPS))
        outs.append(xh * inv)
    return jnp.concatenate(outs, axis=1)


def _rope_blocks(x, nblk, cos, sin):
    lane = lax.broadcasted_iota(jnp.int32, (1, LANES), 1)
    first_half = lane < D_NOPE + D_ROPE // 2
    outs = []
    for h in range(nblk):
        xh = x[:, h * LANES:(h + 1) * LANES]
        partner = jnp.where(first_half, pltpu.roll(xh, LANES - D_ROPE // 2, 1),
                            pltpu.roll(xh, D_ROPE // 2, 1))
        outs.append(xh * cos + partner * sin)
    return jnp.concatenate(outs, axis=1)


def _store_vt(vt_ref, v, t):
    vt = v.T
    for j in range(v.shape[0] // t):
        vt_ref[j] = vt[:, j * t:(j + 1) * t].astype(vt_ref.dtype)


def _odd_in_body(tks, x_ref, pos_ref, g_ref, w_ref, cqn_ref, ckn_ref, qln_ref, wuq_ref, kvn_ref,
                 wukv_ref, dqn_ref, dkn_ref, freq_ref,
                 qc_ref, kc_ref, vc_ref, qd_ref, kd_ref, vd_ref):
    hw = C_HEADS * LANES
    h = _rms(x_ref[...], g_ref[...]).astype(BF16)
    qc = _dot(h, w_ref[:, 0:hw])
    qc_ref[...] = (_half_norm(qc, C_HEADS) * (cqn_ref[...] * (C_DK ** -0.5 * LOG2E))).astype(BF16)
    kc = _dot(h, w_ref[:, hw:2 * hw])
    kc_ref[...] = (_half_norm(kc, C_HEADS) * ckn_ref[...]).astype(BF16)
    _store_vt(vc_ref, _dot(h, w_ref[:, 2 * hw:3 * hw]), tks[0])

    lane = lax.broadcasted_iota(jnp.int32, (1, LANES), 1)
    ang = pos_ref[...].astype(F32) * freq_ref[...]
    is_rope = (lane >= D_NOPE) & (lane < D_QK)
    cos = jnp.where(is_rope, jnp.cos(ang), 1.0)
    sin_raw = jnp.sin(ang)
    sin = jnp.where(is_rope, jnp.where(lane < D_NOPE + D_ROPE // 2, -sin_raw, sin_raw), 0.0)

    off = 3 * hw
    q_lat = _dot(h, w_ref[:, off:off + Q_LORA])
    qd = _dot(_rms(q_lat, qln_ref[...]).astype(BF16), wuq_ref[...])
    qd = _head_norm(qd, D_HEADS, D_QK) * (dqn_ref[...] * (D_QK ** -0.5 * LOG2E))
    qd_ref[...] = _rope_blocks(qd, D_HEADS, cos, sin).astype(BF16)

    off += Q_LORA
    kv_lat = _dot(h, w_ref[:, off:off + KV_LORA])
    k_rope = _dot(h, w_ref[:, off + KV_LORA:off + KV_LORA + LANES])
    kvd = _dot(_rms(kv_lat, kvn_ref[...]).astype(BF16), wukv_ref[...])
    dhw = D_HEADS * LANES
    kd = kvd[:, :dhw] + jnp.concatenate([k_rope] * D_HEADS, axis=1)
    kd = _head_norm(kd, D_HEADS, D_QK) * dkn_ref[...]
    kd_ref[...] = _rope_blocks(kd, D_HEADS, cos, sin).astype(BF16)
    _store_vt(vd_ref, kvd[:, dhw:], tks[1])


def _odd_in(x, pos, g, w, cqn, ckn, qln, wuq, kvn, wukv, dqn, dkn, freq, tm, tks):
    m, d = x.shape
    hw = C_HEADS * LANES
    row = lambda i: (i, 0)
    small = [g, w, cqn, ckn, qln, wuq, kvn, wukv, dqn, dkn, freq]
    rspec = pl.BlockSpec((tm, hw), row)
    tspec = [pl.BlockSpec((tm // t, hw, t), lambda i: (i, 0, 0)) for t in tks]
    rshape = jax.ShapeDtypeStruct((m, hw), BF16)
    tshape = [jax.ShapeDtypeStruct((m // t, hw, t), BF16) for t in tks]
    return pl.pallas_call(
        functools.partial(_odd_in_body, tks),
        grid=(m // tm,),
        in_specs=[pl.BlockSpec((tm, d), row), pl.BlockSpec((tm, 1), row)] + [_full(a.shape) for a in small],
        out_specs=[rspec, rspec, tspec[0], rspec, rspec, tspec[1]],
        out_shape=[rshape, rshape, tshape[0], rshape, rshape, tshape[1]],
        compiler_params=_params("parallel"),
        name="odd_in",
    )(x, pos, *small)


LOG2E = math.log2(math.e)
T_Q = 256
TK_DIFF = 256
TK_MLA = 256


def _flash_body(n_maps, alibi, t, lam_init, *refs):
    if alibi:
        lam_ref, sub_ref, q_ref, k_ref, vt_ref, pq_ref, pk_ref, o_ref = refs
    else:
        q_ref, k_ref, vt_ref, o_ref = refs
    nq = q_ref.shape[0] // t
    nh = q_ref.shape[1] // LANES
    tk = vt_ref.shape[2]
    lane = lax.broadcasted_iota(jnp.int32, (1, LANES), 1)
    key_i = lax.broadcasted_iota(jnp.int32, (tk, t), 0)
    qry_i = lax.broadcasted_iota(jnp.int32, (tk, t), 1)
    shift = CHUNK.bit_length() - 1
    chains = [(h, mp) for h in range(nh) for mp in range(n_maps)]

    def q_block(qi, _):
        rows = pl.ds(pl.multiple_of(qi * t, t), t)
        qm = {}
        for h in range(nh):
            q = q_ref[rows, h * LANES:(h + 1) * LANES]
            for mp in range(n_maps):
                qm[h, mp] = (jnp.where((lane < C_DK) == (mp == 0), q, jnp.zeros_like(q))
                             if n_maps == 2 else q)
        if alibi:
            pq = pq_ref[0, qi].astype(F32)

        def group(g, carry, masked):
            krows = pl.ds(pl.multiple_of(g * tk, tk), tk)
            s = {ch: _dot_nt(k_ref[krows, ch[0] * LANES:(ch[0] + 1) * LANES], qm[ch]) for ch in chains}
            if alibi:
                dist = jnp.abs(pk_ref[krows, :].astype(F32) - pq)
                for ch in chains:
                    s[ch] = s[ch] - (2.0 ** (-2 * (ch[0] + 1)) * LOG2E) * dist
            if masked:
                allowed = ((key_i + g * tk) >> shift) <= ((qry_i + qi * t) >> shift)
                for ch in chains:
                    s[ch] = jnp.where(allowed, s[ch], NEG_INF)
            vt = vt_ref[g]
            new = []
            for ch, (m_i, l_i, acc) in zip(chains, carry):
                m_new = jnp.maximum(m_i, jnp.max(s[ch], axis=0, keepdims=True))
                alpha = jnp.exp2(m_i - m_new)
                p = jnp.exp2(s[ch] - m_new)
                l_new = alpha * l_i + jnp.sum(p, axis=0, keepdims=True)
                pv = _dot(vt[ch[0] * LANES:(ch[0] + 1) * LANES, :], p.astype(BF16))
                new.append((m_new, l_new, alpha * acc + pv))
            return tuple(new)

        init = tuple((jnp.full((1, t), NEG_INF, F32), jnp.zeros((1, t), F32), jnp.zeros((LANES, t), F32))
                     for _ in chains)
        g_last = (qi * t) // tk
        carry = lax.fori_loop(0, g_last, lambda g, c: group(g, c, False), init)
        carry = group(g_last, carry, True)
        outs = {ch: acc / l_i for ch, (_, l_i, acc) in zip(chains, carry)}
        if n_maps == 2:
            lv = lam_ref[...]
            lam = (jnp.exp(jnp.sum(lv[0:1] * lv[1:2], axis=-1, keepdims=True))
                   - jnp.exp(jnp.sum(lv[2:3] * lv[3:4], axis=-1, keepdims=True)) + lam_init)
        for h in range(nh):
            if n_maps == 2:
                o = (outs[h, 0] - lam * outs[h, 1]).T
                o = _rms(o, sub_ref[...]) * (1.0 - lam_init)
            else:
                o = outs[h, 0].T
            o_ref[rows, h * LANES:(h + 1) * LANES] = o.astype(o_ref.dtype)
        return 0

    lax.fori_loop(0, nq, q_block, 0)


def _flash(q, k, vt, batch, t, n_maps, extras=None, lam_init=0.0):
    m, hw = q.shape
    s = m // batch
    nq = s // t
    tk = vt.shape[2]
    alibi = extras is not None
    spec = pl.BlockSpec((s, hw), lambda b: (b, 0))
    in_specs = [spec, spec, pl.BlockSpec((s // tk, hw, tk), lambda b: (b, 0, 0))]
    args = [q, k, vt]
    if alibi:
        lamv, subn, pos_col, pos_row = extras
        in_specs = [_full(lamv.shape), _full(subn.shape)] + in_specs + [
            pl.BlockSpec((1, nq, 1, t), lambda b: (b, 0, 0, 0)),
            pl.BlockSpec((s, 1), lambda b: (b, 0))]
        args = [lamv, subn] + args + [pos_row, pos_col]
    return pl.pallas_call(
        functools.partial(_flash_body, n_maps, alibi, t, lam_init),
        grid=(batch,),
        in_specs=in_specs,
        out_specs=spec,
        out_shape=jax.ShapeDtypeStruct((m, hw), BF16),
        compiler_params=_params("parallel"),
        name="flash_diff" if alibi else "flash_mla",
    )(*args)


def _mem_kv_body(mem_ref, g_ref, w_ref, kn_ref, k_ref, v_ref):
    hw = X_HEADS * X_DH
    mn = _rms(mem_ref[...], g_ref[...]).astype(BF16)
    k = _dot(mn, w_ref[:, :hw])
    k_ref[...] = (_head_norm(k, X_HEADS, X_DH) * kn_ref[...]).astype(BF16)
    v_ref[...] = _dot(mn, w_ref[:, hw:]).astype(BF16)


def _mem_kv(mem2, g, w, kn, batch):
    rows, d = mem2.shape
    nm = rows // batch
    hw = X_HEADS * X_DH
    return pl.pallas_call(
        _mem_kv_body,
        grid=(batch,),
        in_specs=[pl.BlockSpec((nm, d), lambda b: (b, 0)), _full(g.shape), _full(w.shape), _full(kn.shape)],
        out_specs=[pl.BlockSpec((nm, hw), lambda b: (b, 0))] * 2,
        out_shape=[jax.ShapeDtypeStruct((rows, hw), BF16)] * 2,
        compiler_params=_params("parallel"),
        name="mem_kv",
    )(mem2, g, w, kn)


def _cross_update(x, g_ref, wq_ref, qn_ref, k_ref, v_ref, wo_ref):
    q = _dot(_rms(x, g_ref[...]).astype(BF16), wq_ref[...])
    q = (_head_norm(q, X_HEADS, X_DH) * (qn_ref[...] * X_DH ** -0.5)).astype(BF16)
    outs = []
    for h in range(X_HEADS):
        blk = slice(h * X_DH, (h + 1) * X_DH)
        s = _dot_nt(q[:, blk], k_ref[:, blk])
        p = jnp.exp(s - jnp.max(s, axis=-1, keepdims=True))
        l = jnp.sum(p, axis=-1, keepdims=True)
        outs.append((_dot(p.astype(BF16), v_ref[:, blk]) / l).astype(BF16))
    return x + _dot(jnp.concatenate(outs, axis=1), wo_ref[...])


def _ffn_update(x, g_ref, wi_ref, wo_ref, hid, th):
    h = _rms(x, g_ref[...]).astype(BF16)
    acc = x
    for c in range(hid // th):
        gate = _dot(h, wi_ref[:, c * th:(c + 1) * th])
        up = _dot(h, wi_ref[:, hid + c * th:hid + (c + 1) * th])
        acc = acc + _dot((_silu(gate) * up).astype(BF16), wo_ref[c * th:(c + 1) * th, :])
    return acc


def _post_body(ka, hid, th, x_ref, a_ref, b_ref, wmix_ref, gx_ref, wq_ref, qn_ref, k_ref, v_ref,
               wo_ref, gf_ref, wi_ref, wf_ref, o_ref):
    x = x_ref[...] + _dot(a_ref[...], wmix_ref[:ka, :]) + _dot(b_ref[...], wmix_ref[ka:, :])
    x = _cross_update(x, gx_ref, wq_ref, qn_ref, k_ref, v_ref, wo_ref)
    o_ref[...] = _ffn_update(x, gf_ref, wi_ref, wf_ref, hid, th)


def _resident(shape):
    return pl.BlockSpec(shape, lambda *_: (0,) * len(shape), pipeline_mode=pl.Buffered(1))


def _post_mixer(x, a, b, wmix, gx, wq, qn, k, v, wo, gf, wi, wf, batch, tm, th):
    m, d = x.shape
    ka, kb = a.shape[1], b.shape[1]
    per = (m // batch) // tm
    nm = k.shape[0] // batch
    hw = X_HEADS * X_DH
    hid = wf.shape[0]
    row = lambda i: (i, 0)
    kv = pl.BlockSpec((nm, hw), lambda i: (i // per, 0))
    return pl.pallas_call(
        functools.partial(_post_body, ka, hid, th),
        grid=(m // tm,),
        in_specs=[pl.BlockSpec((tm, d), row), pl.BlockSpec((tm, ka), row), pl.BlockSpec((tm, kb), row),
                  _resident(wmix.shape), _full(gx.shape), _resident(wq.shape), _full(qn.shape), kv, kv,
                  _resident(wo.shape), _full(gf.shape), _resident(wi.shape), _resident(wf.shape)],
        out_specs=pl.BlockSpec((tm, d), row),
        out_shape=jax.ShapeDtypeStruct((m, d), F32),
        compiler_params=_params("parallel"),
        name="post_mixer",
    )(x, a, b, wmix, gx, wq, qn, k, v, wo, gf, wi, wf)


def _row(v):
    return v.reshape(1, -1).astype(F32)


def _pad_cols(w, width):
    return jnp.pad(w, ((0, 0), (0, width - w.shape[1])))


def _block_diag(w):
    n, d, e = w.shape
    eye = jnp.eye(n, dtype=w.dtype)
    return (eye[:, None, :, None] * w[:, :, None, :]).reshape(n * d, n * e)


def _even_layer(x, batch, tm, tb, norm_g, w_in, conv_qkv, a_log, dt_bias, o_norm, conv_b_w, conv_b_b,
                gate_a_w, gate_a_b, gate_x_w, gate_x_b, lru_l):
    qk = A_HEADS * A_D
    c_qkv, c_z = 3 * qk, qk
    o = 0
    w_qkv = w_in[:, o:o + c_qkv]; o += c_qkv
    w_z = w_in[:, o:o + c_z]; o += c_z
    w_bd = w_in[:, o:o + 2 * A_HEADS]; o += 2 * A_HEADS
    w_xb = w_in[:, o:o + B_WIDTH]; o += B_WIDTH
    w_gb = w_in[:, o:o + B_WIDTH]
    w = jnp.concatenate([w_qkv, w_z, w_xb, w_gb, _pad_cols(w_bd, LANES)], axis=1).astype(BF16)
    qkv, z, xb, gb, bd = _norm_proj(x, _row(norm_g), w, [c_qkv, c_z, B_WIDTH, B_WIDTH, LANES],
                                    [F32] * 5, tm)
    pad = jnp.zeros((A_HEADS,), F32)
    avec = _row(jnp.pad(jnp.concatenate([pad, a_log.astype(F32)]), (0, LANES - 2 * A_HEADS)))
    dvec = _row(jnp.pad(jnp.concatenate([pad, dt_bias.astype(F32)]), (0, LANES - 2 * A_HEADS)))
    y_a = _delta_rule(qkv, z, bd, conv_qkv.astype(F32), avec, dvec, _row(o_norm), batch, tb)
    y_b = _rglru(xb, gb, conv_b_w.astype(F32), _row(conv_b_b), _block_diag(gate_a_w).astype(BF16),
                 _row(gate_a_b), _block_diag(gate_x_w).astype(BF16), _row(gate_x_b), _row(lru_l),
                 batch, tb)
    return y_a, y_b


def _head_pad_cols(w, heads, real, total):
    k = w.shape[0]
    return jnp.pad(w.reshape(k, heads, real), ((0, 0), (0, 0), (0, total - real))).reshape(k, heads * total)


def _odd_layer(x, pos_col, pos_row, batch, tm, tq, layer_idx, norm_g, w_in, c_q_norm, c_k_norm,
               lam_q1, lam_k1, lam_q2, lam_k2, c_sub_norm, q_lat_norm, w_uq, kv_lat_norm, w_ukv,
               d_q_norm, d_k_norm):
    hw = C_HEADS * 2 * C_DK
    k_rope_w = jnp.pad(w_in[:, 3 * hw + Q_LORA + KV_LORA:], ((0, 0), (D_NOPE, LANES - D_QK)))
    w = jnp.concatenate([w_in[:, :3 * hw + Q_LORA + KV_LORA], k_rope_w], axis=1).astype(BF16)
    wuq = _head_pad_cols(w_uq, D_HEADS, D_QK, LANES).astype(BF16)
    kvr = w_ukv.reshape(KV_LORA, D_HEADS, D_NOPE + LANES)
    wukv = jnp.concatenate([_head_pad_cols(kvr[:, :, :D_NOPE].reshape(KV_LORA, -1), D_HEADS, D_NOPE, LANES),
                            kvr[:, :, D_NOPE:].reshape(KV_LORA, -1)], axis=1).astype(BF16)
    tile = lambda v, n: _row(jnp.tile(v.astype(F32), n))
    padn = lambda v: jnp.pad(v.astype(F32), (0, LANES - D_QK))
    half = D_ROPE // 2
    inv_freq = ROPE_THETA ** (-jnp.arange(half, dtype=F32) / half)
    freq = _row(jnp.concatenate([jnp.zeros((D_NOPE,), F32), inv_freq, inv_freq,
                                 jnp.zeros((LANES - D_QK,), F32)]))
    qc, kc, vc, qd, kd, vd = _odd_in(
        x, pos_col, _row(norm_g), w, tile(c_q_norm, 2 * C_HEADS), tile(c_k_norm, 2 * C_HEADS),
        _row(q_lat_norm), wuq, _row(kv_lat_norm), wukv, tile(padn(d_q_norm), D_HEADS),
        tile(padn(d_k_norm), D_HEADS), freq, tm, (min(TK_DIFF, tm), min(TK_MLA, tm)))
    lam_init = 0.8 - 0.6 * math.exp(-0.3 * layer_idx)
    lamv = jnp.stack([lam_q1, lam_k1, lam_q2, lam_k2]).astype(F32)
    yc = _flash(qc, kc, vc, batch, tq, 2, (lamv, _row(c_sub_norm), pos_col, pos_row), lam_init)
    yd = _flash(qd, kd, vd, batch, tq, 1)
    return yc, yd


def kernel(x, mem, positions, norm_mix, norm_x, norm_mem, x_wq, x_wkv, x_q_norm, x_k_norm, x_wo, norm_ffn, ffn_w_in, ffn_w_out, ev_w_in, ev_conv_qkv, ev_a_log, ev_dt_bias, ev_o_norm, ev_conv_b_w, ev_conv_b_b, ev_gate_a_w, ev_gate_a_b, ev_gate_x_w, ev_gate_x_b, ev_lru_l, ev_w_out, od_w_in, od_c_q_norm, od_c_k_norm, od_lam_q1, od_lam_k1, od_lam_q2, od_lam_k2, od_c_sub_norm, od_q_lat_norm, od_w_uq, od_kv_lat_norm, od_w_ukv, od_d_q_norm, od_d_k_norm, od_w_out):
    batch, seq, d = x.shape
    depth = norm_mix.shape[0]
    m = batch * seq
    tm = min(512, seq)
    tb = min(256, seq)
    tq = min(T_Q, seq)
    th = ffn_w_out.shape[1]
    th = 1408 if th % 1408 == 0 else th
    xf = x.reshape(m, d).astype(F32)
    mem2 = mem.reshape(-1, d).astype(F32)
    pos_col = positions.reshape(m, 1).astype(jnp.int32)
    pos_row = positions.reshape(batch, seq // tq, 1, tq).astype(jnp.int32)
    for l in range(depth):
        i = l // 2
        if l % 2 == 0:
            y_a, y_b = _even_layer(xf, batch, tm, tb, norm_mix[l], ev_w_in[i], ev_conv_qkv[i], ev_a_log[i],
                                   ev_dt_bias[i], ev_o_norm[i], ev_conv_b_w[i], ev_conv_b_b[i],
                                   ev_gate_a_w[i], ev_gate_a_b[i], ev_gate_x_w[i], ev_gate_x_b[i],
                                   ev_lru_l[i])
            w_mix = ev_w_out[i]
        else:
            y_a, y_b = _odd_layer(xf, pos_col, pos_row, batch, tm, tq, l, norm_mix[l], od_w_in[i],
                                  od_c_q_norm[i], od_c_k_norm[i], od_lam_q1[i], od_lam_k1[i], od_lam_q2[i],
                                  od_lam_k2[i], od_c_sub_norm[i], od_q_lat_norm[i], od_w_uq[i],
                                  od_kv_lat_norm[i], od_w_ukv[i], od_d_q_norm[i], od_d_k_norm[i])
            w_mix = od_w_out[i]
        k_mem, v_mem = _mem_kv(mem2, _row(norm_mem[l]), x_wkv[l].astype(BF16),
                               _row(jnp.tile(x_k_norm[l], X_HEADS)), batch)
        xf = _post_mixer(xf, y_a, y_b, w_mix.astype(BF16), _row(norm_x[l]), x_wq[l].astype(BF16),
                         _row(jnp.tile(x_q_norm[l], X_HEADS)), k_mem, v_mem, x_wo[l].astype(BF16),
                         _row(norm_ffn[l]), ffn_w_in[l].astype(BF16), ffn_w_out[l].astype(BF16),
                         batch, tm, th)
    return xf.reshape(batch, seq, d).astype(x.dtype)
```

```python
import functools
import math
from typing import NamedTuple

import jax
import jax.numpy as jnp
import numpy as np
from jax import lax
from jax.experimental import pallas as pl
from jax.experimental.pallas import tpu as pltpu

F32 = jnp.float32
BF16 = jnp.bfloat16

NORM_EPS = 1e-6
NEG_INF = -1e30
CHUNK = 64
LANES = 128
LRU_C = 8.0
ROPE_THETA = 10000.0
VMEM_LIMIT = 56 * 1024 * 1024

A_HEADS = 4
A_D = 128
B_WIDTH = 512
B_BLOCKS = 8
C_HEADS = 4
C_DK = 64
D_HEADS = 4
D_NOPE = 64
D_ROPE = 32
D_QK = D_NOPE + D_ROPE
Q_LORA = 256
KV_LORA = 128
X_HEADS = 4
X_DH = 128


def _params(*sem):
    return pltpu.CompilerParams(dimension_semantics=sem, vmem_limit_bytes=VMEM_LIMIT)


def _rms(x, g):
    ms = jnp.mean(x * x, axis=-1, keepdims=True)
    return x * lax.rsqrt(ms + NORM_EPS) * g


def _dot(a, b):
    return jnp.dot(a, b, preferred_element_type=F32)


def _dot_nt(a, b):
    return lax.dot_general(a, b, (((1,), (1,)), ((), ())), preferred_element_type=F32)


def _sigmoid(x):
    return 1.0 / (1.0 + jnp.exp(-x))


def _silu(x):
    return x * _sigmoid(x)


def _softplus(x):
    return jnp.maximum(x, 0.0) + jnp.log(1.0 + jnp.exp(-jnp.abs(x)))


def _full(shape):
    return pl.BlockSpec(shape, lambda *_: (0,) * len(shape))


class _Layer(NamedTuple):
    arr: jax.Array
    idx: int


def _whole(a, resident=False):
    kw = dict(pipeline_mode=pl.Buffered(1)) if resident else {}
    if isinstance(a, _Layer):
        tail = a.arr.shape[1:]
        return a.arr, pl.BlockSpec((None,) + tail, lambda *_: (a.idx,) + (0,) * len(tail), **kw)
    return a, pl.BlockSpec(a.shape, lambda *_: (0,) * a.ndim, **kw)


def _call(body, grid, tiled, whole, out_specs, out_shape, sem, name, scratch=()):
    ops = list(tiled) + list(whole)
    return pl.pallas_call(
        body, grid=grid, in_specs=[s for _, s in ops], out_specs=out_specs, out_shape=out_shape,
        scratch_shapes=list(scratch), compiler_params=_params(*sem), name=name,
    )(*[a for a, _ in ops])


def _norm_proj_body(splits, x_ref, g_ref, w_ref, *out_refs):
    h = _rms(x_ref[...], g_ref[...]).astype(BF16)
    for (start, width), o_ref in zip(splits, out_refs):
        o_ref[...] = _dot(h, w_ref[:, start:start + width]).astype(o_ref.dtype)


def _norm_proj(x, g, w, widths, dtypes, tm):
    m, d = x.shape
    starts = np.concatenate([[0], np.cumsum(widths)[:-1]]).tolist()
    splits = tuple(zip(starts, widths))
    return _call(
        functools.partial(_norm_proj_body, splits), (m // tm,),
        [(x, pl.BlockSpec((tm, d), lambda i: (i, 0)))], [_whole(g), _whole(w)],
        [pl.BlockSpec((tm, wd), lambda i: (i, 0)) for wd in widths],
        [jax.ShapeDtypeStruct((m, wd), dt) for wd, dt in zip(widths, dtypes)],
        ("parallel",), "norm_proj")


HALO = 8


def _causal_conv(x, w_ref, buf_ref, hist_ref, first):
    tb = x.shape[0]
    k = w_ref.shape[0]

    @pl.when(first)
    def _():
        hist_ref[...] = jnp.zeros_like(hist_ref)

    buf_ref[0:HALO, :] = hist_ref[...]
    buf_ref[HALO:, :] = x
    hist_ref[...] = x[tb - HALO:, :]
    y = buf_ref[pl.ds(HALO - (k - 1), tb), :] * w_ref[0:1, :]
    for j in range(1, k):
        y = y + buf_ref[pl.ds(HALO - (k - 1) + j, tb), :] * w_ref[j:j + 1, :]
    return y


def _delta_body(tb, qkv_ref, z_ref, bd_ref, cw_ref, avec_ref, dvec_ref, onorm_ref, tri_ref,
                y_ref, buf_ref, hist_ref, state_ref):
    first = pl.program_id(1) == 0
    nh, d = A_HEADS, A_D
    qk_w = nh * d

    @pl.when(first)
    def _():
        state_ref[...] = jnp.zeros_like(state_ref)

    act = _silu(_causal_conv(qkv_ref[...], cw_ref, buf_ref, hist_ref, first))

    bd = bd_ref[...]
    beta_all = _sigmoid(bd)
    g_all = -jnp.exp(avec_ref[...]) * _softplus(bd + dvec_ref[...])
    cum_all = jnp.dot(tri_ref[...], g_all, precision=lax.Precision.HIGHEST,
                      preferred_element_type=F32)

    cum_t = cum_all.T

    ri = lax.broadcasted_iota(jnp.int32, (CHUNK, CHUNK), 0)
    ci = lax.broadcasted_iota(jnp.int32, (CHUNK, CHUNK), 1)
    incl = ri >= ci
    strict = ri > ci
    eye = jnp.where(ri == ci, 1.0, 0.0).astype(F32)
    scale = d ** -0.5

    nc = tb // CHUNK
    probs = [(h, c) for c in range(nc) for h in range(nh)]
    qs, ks, vs = {}, {}, {}
    for h in range(nh):
        q_h = act[:, h * d:(h + 1) * d]
        k_h = act[:, qk_w + h * d:qk_w + (h + 1) * d]
        q_h = q_h * (lax.rsqrt(jnp.sum(q_h * q_h, axis=-1, keepdims=True) + NORM_EPS) * scale)
        k_h = k_h * lax.rsqrt(jnp.sum(k_h * k_h, axis=-1, keepdims=True) + NORM_EPS)
        for c in range(nc):
            rows = slice(c * CHUNK, (c + 1) * CHUNK)
            qs[h, c] = q_h[rows]
            ks[h, c] = k_h[rows]
            vs[h, c] = act[rows, 2 * qk_w + h * d:2 * qk_w + (h + 1) * d]

    a_mat, rhs_wu, q_dec, k_dec_t, qk, neg_m, last = {}, {}, {}, {}, {}, {}, {}
    for pr in probs:
        h, c = pr
        rows = slice(c * CHUNK, (c + 1) * CHUNK)
        q, k, v = qs[pr], ks[pr], vs[pr]
        beta = beta_all[rows, h:h + 1]
        cum = cum_all[rows, nh + h:nh + h + 1]
        cum_row = cum_t[nh + h:nh + h + 1, c * CHUNK:(c + 1) * CHUNK]
        cum_last = cum[CHUNK - 1:CHUNK, :]
        decay = jnp.where(incl, jnp.exp(jnp.where(incl, cum - cum_row, 0.0)), 0.0)
        e_cum = jnp.exp(cum)
        kb = k * beta
        a_mat[pr] = _dot_nt(jnp.concatenate([q, kb], axis=0).astype(BF16), k.astype(BF16))
        qk[pr] = jnp.where(incl, a_mat[pr][:CHUNK] * decay, 0.0).astype(BF16)
        neg_m[pr] = jnp.where(strict, -a_mat[pr][CHUNK:] * decay, 0.0)
        rhs_wu[pr] = jnp.concatenate([kb * e_cum, v * beta], axis=1).astype(BF16)
        q_dec[pr] = q * e_cum
        k_dec_t[pr] = (k * jnp.exp(cum_last - cum)).T.astype(BF16)
        last[pr] = jnp.exp(cum_last)
    pw = dict(neg_m)
    t_inv = {pr: eye + neg_m[pr] for pr in probs}
    for _ in range(CHUNK.bit_length() - 2):
        for pr in probs:
            p_bf = pw[pr].astype(BF16)
            pw[pr] = _dot(p_bf, p_bf)
        for pr in probs:
            t_inv[pr] = _dot(t_inv[pr].astype(BF16), (eye + pw[pr]).astype(BF16))
    wu = {pr: _dot(t_inv[pr].astype(BF16), rhs_wu[pr]) for pr in probs}

    state = [state_ref[h] for h in range(nh)]
    outs = [[] for _ in range(nh)]
    for c in range(nc):
        ws = [_dot(jnp.concatenate([wu[h, c][:, :d], q_dec[h, c]], axis=0).astype(BF16),
                   state[h].astype(BF16)) for h in range(nh)]
        v_new = [(wu[h, c][:, d:] - ws[h][:CHUNK]).astype(BF16) for h in range(nh)]
        for h in range(nh):
            outs[h].append(ws[h][CHUNK:] + _dot(qk[h, c], v_new[h]))
        state = [state[h] * last[h, c] + _dot(k_dec_t[h, c], v_new[h]) for h in range(nh)]
    for h in range(nh):
        state_ref[h] = state[h]
        o = jnp.concatenate(outs[h], axis=0)
        o = _rms(o, onorm_ref[...]) * _silu(z_ref[:, h * d:(h + 1) * d])
        y_ref[:, h * d:(h + 1) * d] = o.astype(y_ref.dtype)


def _delta_rule(qkv, z, bd, conv_w, avec, dvec, onorm, batch, tb):
    m, c3 = qkv.shape
    s = m // batch
    nsb = s // tb
    cid = np.arange(tb) // CHUNK
    tri = jnp.asarray(((cid[:, None] == cid[None, :]) &
                       (np.arange(tb)[:, None] >= np.arange(tb)[None, :])).astype(np.float32))
    vw = A_HEADS * A_D
    row = lambda b, j: (b * nsb + j, 0)
    return pl.pallas_call(
        functools.partial(_delta_body, tb),
        grid=(batch, nsb),
        in_specs=[pl.BlockSpec((tb, c3), row), pl.BlockSpec((tb, vw), row),
                  pl.BlockSpec((tb, LANES), row), _full(conv_w.shape), _full(avec.shape),
                  _full(dvec.shape), _full(onorm.shape), _full(tri.shape)],
        out_specs=pl.BlockSpec((tb, vw), row),
        out_shape=jax.ShapeDtypeStruct((m, vw), BF16),
        scratch_shapes=[pltpu.VMEM((HALO + tb, c3), F32), pltpu.VMEM((HALO, c3), F32),
                        pltpu.VMEM((A_HEADS, A_D, A_D), F32)],
        compiler_params=_params("parallel", "arbitrary"),
        name="delta_rule",
    )(qkv, z, bd, conv_w, avec, dvec, onorm, tri)


def _gelu_tanh(x):
    return 0.5 * x * (1.0 + jnp.tanh(math.sqrt(2.0 / math.pi) * (x + 0.044715 * (x * x * x))))


def _lru_body(tb, xb_ref, gb_ref, cw_ref, cb_ref, wa_ref, ba_ref, wx_ref, bx_ref, l_ref,
              y_ref, buf_ref, hist_ref, h_ref):
    first = pl.program_id(1) == 0

    @pl.when(first)
    def _():
        h_ref[...] = jnp.zeros_like(h_ref)

    xc = _causal_conv(xb_ref[...], cw_ref, buf_ref, hist_ref, first) + cb_ref[...]
    xc_bf = xc.astype(BF16)
    r = _sigmoid(_dot(xc_bf, wa_ref[...]) + ba_ref[...])
    i = _sigmoid(_dot(xc_bf, wx_ref[...]) + bx_ref[...])
    log_a = (-LRU_C) * r * _softplus(-l_ref[...])
    a = jnp.exp(log_a)
    th = jnp.tanh(log_a)
    u = jnp.sqrt(-2.0 * th / (1.0 - th)) * (i * xc)
    rows = lax.broadcasted_iota(jnp.int32, a.shape, 0)
    shift = 1
    while shift < tb:
        keep = rows >= shift
        a_prev = jnp.where(keep, pltpu.roll(a, shift, 0), 1.0)
        u_prev = jnp.where(keep, pltpu.roll(u, shift, 0), 0.0)
        u = u + a * u_prev
        a = a * a_prev
        shift *= 2
    hs = u + a * h_ref[...]
    h_ref[...] = hs[tb - 1:tb, :]
    y_ref[...] = (_gelu_tanh(gb_ref[...]) * hs).astype(y_ref.dtype)


def _rglru(xb, gb, cw, cb, wa, ba, wx, bx, lru_l, batch, tb):
    m, wdt = xb.shape
    nsb = (m // batch) // tb
    row = lambda b, j: (b * nsb + j, 0)
    return pl.pallas_call(
        functools.partial(_lru_body, tb),
        grid=(batch, nsb),
        in_specs=[pl.BlockSpec((tb, wdt), row), pl.BlockSpec((tb, wdt), row),
                  _full(cw.shape), _full(cb.shape), _full(wa.shape), _full(ba.shape),
                  _full(wx.shape), _full(bx.shape), _full(lru_l.shape)],
        out_specs=pl.BlockSpec((tb, wdt), row),
        out_shape=jax.ShapeDtypeStruct((m, wdt), BF16),
        scratch_shapes=[pltpu.VMEM((HALO + tb, wdt), F32), pltpu.VMEM((HALO, wdt), F32),
                        pltpu.VMEM((1, wdt), F32)],
        compiler_params=_params("parallel", "arbitrary"),
        name="rglru",
    )(xb, gb, cw, cb, wa, ba, wx, bx, lru_l)


def _head_norm(x, nblk, denom):
    outs = []
    for h in range(nblk):
        xh = x[:, h * LANES:(h + 1) * LANES]
        ss = jnp.sum(xh * xh, axis=-1, keepdims=True)
        outs.append(xh * lax.rsqrt(ss * (1.0 / denom) + NORM_EPS))
    return jnp.concatenate(outs, axis=1)


def _half_norm(x, nblk):
    lane = lax.broadcasted_iota(jnp.int32, (1, LANES), 1)
    lo = lane < C_DK
    outs = []
    for h in range(nblk):
        xh = x[:, h * LANES:(h + 1) * LANES]
        sq = xh * xh
        s_lo = jnp.sum(jnp.where(lo, sq, 0.0), axis=-1, keepdims=True)
        s_hi = jnp.sum(jnp.where(lo, 0.0, sq), axis=-1, keepdims=True)
        inv = jnp.where(lo, lax.rsqrt(s_lo * (1.0 / C_DK) + NORM_EPS),
                        lax.rsqrt(s_hi * (1.0 / C_DK) + NORM_EPS))
        outs.append(xh * inv)
    return jnp.concatenate(outs, axis=1)


def _rope_blocks(x, nblk, cos, sin):
    lane = lax.broadcasted_iota(jnp.int32, (1, LANES), 1)
    first_half = lane < D_NOPE + D_ROPE // 2
    outs = []
    for h in range(nblk):
        xh = x[:, h * LANES:(h + 1) * LANES]
        partner = jnp.where(first_half, pltpu.roll(xh, LANES - D_ROPE // 2, 1),
                            pltpu.roll(xh, D_ROPE // 2, 1))
        outs.append(xh * cos + partner * sin)
    return jnp.concatenate(outs, axis=1)


def _store_vt(vt_ref, v, t):
    vt = v.T
    for j in range(v.shape[0] // t):
        vt_ref[j] = vt[:, j * t:(j + 1) * t].astype(vt_ref.dtype)


def _odd_in_body(tks, x_ref, pos_ref, g_ref, w_ref, cqn_ref, ckn_ref, qln_ref, wuq_ref, kvn_ref,
                 wukv_ref, dqn_ref, dkn_ref, freq_ref,
                 qc_ref, kc_ref, vc_ref, qd_ref, kd_ref, vd_ref):
    hw = C_HEADS * LANES
    h = _rms(x_ref[...], g_ref[...]).astype(BF16)
    qc = _dot(h, w_ref[:, 0:hw])
    qc_ref[...] = (_half_norm(qc, C_HEADS) * (cqn_ref[...] * (C_DK ** -0.5 * LOG2E))).astype(BF16)
    kc = _dot(h, w_ref[:, hw:2 * hw])
    kc_ref[...] = (_half_norm(kc, C_HEADS) * ckn_ref[...]).astype(BF16)
    _store_vt(vc_ref, _dot(h, w_ref[:, 2 * hw:3 * hw]), tks[0])

    lane = lax.broadcasted_iota(jnp.int32, (1, LANES), 1)
    ang = pos_ref[...].astype(F32) * freq_ref[...]
    is_rope = (lane >= D_NOPE) & (lane < D_QK)
    cos = jnp.where(is_rope, jnp.cos(ang), 1.0)
    sin_raw = jnp.sin(ang)
    sin = jnp.where(is_rope, jnp.where(lane < D_NOPE + D_ROPE // 2, -sin_raw, sin_raw), 0.0)

    off = 3 * hw
    q_lat = _dot(h, w_ref[:, off:off + Q_LORA])
    qd = _dot(_rms(q_lat, qln_ref[...]).astype(BF16), wuq_ref[...])
    qd = _head_norm(qd, D_HEADS, D_QK) * (dqn_ref[...] * (D_QK ** -0.5 * LOG2E))
    qd_ref[...] = _rope_blocks(qd, D_HEADS, cos, sin).astype(BF16)

    off += Q_LORA
    kv_lat = _dot(h, w_ref[:, off:off + KV_LORA])
    k_rope = _dot(h, w_ref[:, off + KV_LORA:off + KV_LORA + LANES])
    kvd = _dot(_rms(kv_lat, kvn_ref[...]).astype(BF16), wukv_ref[...])
    dhw = D_HEADS * LANES
    kd = kvd[:, :dhw] + jnp.concatenate([k_rope] * D_HEADS, axis=1)
    kd = _head_norm(kd, D_HEADS, D_QK) * dkn_ref[...]
    kd_ref[...] = _rope_blocks(kd, D_HEADS, cos, sin).astype(BF16)
    _store_vt(vd_ref, kvd[:, dhw:], tks[1])


def _odd_in(x, pos, g, w, cqn, ckn, qln, wuq, kvn, wukv, dqn, dkn, freq, tm, tks):
    m, d = x.shape
    hw = C_HEADS * LANES
    row = lambda i: (i, 0)
    small = [g, w, cqn, ckn, qln, wuq, kvn, wukv, dqn, dkn, freq]
    rspec = pl.BlockSpec((tm, hw), row)
    tspec = [pl.BlockSpec((tm // t, hw, t), lambda i: (i, 0, 0)) for t in tks]
    rshape = jax.ShapeDtypeStruct((m, hw), BF16)
    tshape = [jax.ShapeDtypeStruct((m // t, hw, t), BF16) for t in tks]
    return _call(
        functools.partial(_odd_in_body, tks), (m // tm,),
        [(x, pl.BlockSpec((tm, d), row)), (pos, pl.BlockSpec((tm, 1), row))],
        [_whole(a) for a in small],
        [rspec, rspec, tspec[0], rspec, rspec, tspec[1]],
        [rshape, rshape, tshape[0], rshape, rshape, tshape[1]],
        ("parallel",), "odd_in")


LOG2E = math.log2(math.e)
T_Q = 256
TK_DIFF = 256
TK_MLA = 256
SUM_ROWS = 16
NB_MLA = 2


def _flash_body(n_maps, alibi, t, nb, lam_init, *refs):
    if alibi:
        lam_ref, sub_ref, q_ref, k_ref, vt_ref, pq_ref, pk_ref, o_ref = refs
    else:
        q_ref, k_ref, vt_ref, o_ref = refs
    seq = q_ref.shape[0] // nb
    nq = seq // t
    nh = q_ref.shape[1] // LANES
    tk = vt_ref.shape[2]
    ng = seq // tk
    lane = lax.broadcasted_iota(jnp.int32, (1, LANES), 1)
    key_i = lax.broadcasted_iota(jnp.int32, (tk, t), 0)
    qry_i = lax.broadcasted_iota(jnp.int32, (tk, t), 1)
    shift = CHUNK.bit_length() - 1
    chains = [(b, h, mp) for b in range(nb) for h in range(nh) for mp in range(n_maps)]
    ones_rows = jnp.ones((SUM_ROWS, tk), BF16)
    hcols = lambda h: slice(h * LANES, (h + 1) * LANES)

    def q_block(qi, _):
        qm = {}
        for b in range(nb):
            rows = pl.ds(pl.multiple_of(b * seq + qi * t, t), t)
            for h in range(nh):
                q = q_ref[rows, hcols(h)]
                for mp in range(n_maps):
                    qm[b, h, mp] = (jnp.where((lane < C_DK) == (mp == 0), q, jnp.zeros_like(q))
                                    if n_maps == 2 else q)

        def group(g, carry, masked):
            krows = [pl.ds(pl.multiple_of(b * seq + g * tk, tk), tk) for b in range(nb)]
            s = {ch: _dot_nt(k_ref[krows[ch[0]], hcols(ch[1])], qm[ch]) for ch in chains}
            if alibi:
                for b in range(nb):
                    dist = jnp.abs(pk_ref[krows[b], :].astype(F32) - pq_ref[b, qi].astype(F32))
                    for ch in chains:
                        if ch[0] == b:
                            s[ch] = s[ch] - (2.0 ** (-2 * (ch[1] + 1)) * LOG2E) * dist
            if masked:
                allowed = ((key_i + g * tk) >> shift) <= ((qry_i + qi * t) >> shift)
                for ch in chains:
                    s[ch] = jnp.where(allowed, s[ch], NEG_INF)
            vt_ext = {(b, h): jnp.concatenate([vt_ref[b * ng + g, hcols(h), :], ones_rows], axis=0)
                      for b in range(nb) for h in range(nh)}
            new = []
            for ch, (m_i, acc) in zip(chains, carry):
                m_new = jnp.maximum(m_i, jnp.max(s[ch], axis=0, keepdims=True))
                alpha = jnp.exp2(m_i - m_new)
                p = jnp.exp2((s[ch] - m_new).astype(BF16))
                new.append((m_new, alpha * acc + _dot(vt_ext[ch[:2]], p)))
            return tuple(new)

        init = tuple((jnp.full((1, t), NEG_INF, F32), jnp.zeros((LANES + SUM_ROWS, t), F32))
                     for _ in chains)
        g_last = (qi * t) // tk
        carry = lax.fori_loop(0, g_last, lambda g, c: group(g, c, False), init)
        carry = group(g_last, carry, True)
        outs = {ch: acc[:LANES] / acc[LANES:LANES + 1] for ch, (_, acc) in zip(chains, carry)}
        if n_maps == 2:
            lv = lam_ref[...]
            lam = (jnp.exp(jnp.sum(lv[0:1] * lv[1:2], axis=-1, keepdims=True))
                   - jnp.exp(jnp.sum(lv[2:3] * lv[3:4], axis=-1, keepdims=True)) + lam_init)
        for b in range(nb):
            rows = pl.ds(pl.multiple_of(b * seq + qi * t, t), t)
            for h in range(nh):
                if n_maps == 2:
                    o = (outs[b, h, 0] - lam * outs[b, h, 1]).T
                    o = _rms(o, sub_ref[...]) * (1.0 - lam_init)
                else:
                    o = outs[b, h, 0].T
                o_ref[rows, hcols(h)] = o.astype(o_ref.dtype)
        return 0

    lax.fori_loop(0, nq, q_block, 0)


def _flash(q, k, vt, batch, t, nb, n_maps, extras=None, lam_init=0.0):
    m, hw = q.shape
    s = m // batch
    nq = s // t
    tk = vt.shape[2]
    alibi = extras is not None
    spec = pl.BlockSpec((nb * s, hw), lambda b: (b, 0))
    in_specs = [spec, spec, pl.BlockSpec((nb * s // tk, hw, tk), lambda b: (b, 0, 0))]
    args = [q, k, vt]
    if alibi:
        lamv, subn, pos_col, pos_row = extras
        in_specs = [_full(lamv.shape), _full(subn.shape)] + in_specs + [
            pl.BlockSpec((nb, nq, 1, t), lambda b: (b, 0, 0, 0)),
            pl.BlockSpec((nb * s, 1), lambda b: (b, 0))]
        args = [lamv, subn] + args + [pos_row, pos_col]
    return pl.pallas_call(
        functools.partial(_flash_body, n_maps, alibi, t, nb, lam_init),
        grid=(batch // nb,),
        in_specs=in_specs,
        out_specs=spec,
        out_shape=jax.ShapeDtypeStruct((m, hw), BF16),
        compiler_params=_params("parallel"),
        name="flash_diff" if alibi else "flash_mla",
    )(*args)


def _mem_kv_body(mem_ref, g_ref, w_ref, kn_ref, k_ref, v_ref):
    hw = X_HEADS * X_DH
    mn = _rms(mem_ref[...], g_ref[...]).astype(BF16)
    k = _dot(mn, w_ref[:, :hw])
    k_ref[...] = (_head_norm(k, X_HEADS, X_DH) * kn_ref[...]).astype(BF16)
    v_ref[...] = _dot(mn, w_ref[:, hw:]).astype(BF16)


def _mem_kv(mem2, g, w, kn, batch):
    rows, d = mem2.shape
    nm = rows // batch
    hw = X_HEADS * X_DH
    return _call(
        _mem_kv_body, (batch,),
        [(mem2, pl.BlockSpec((nm, d), lambda b: (b, 0)))],
        [_whole(g), _whole(w), _whole(kn)],
        [pl.BlockSpec((nm, hw), lambda b: (b, 0))] * 2,
        [jax.ShapeDtypeStruct((rows, hw), BF16)] * 2,
        ("parallel",), "mem_kv")


def _cross_update(x, g_ref, wq_ref, qn_ref, k_ref, v_ref, wo_ref):
    q = _dot(_rms(x, g_ref[...]).astype(BF16), wq_ref[...])
    q = (_head_norm(q, X_HEADS, X_DH) * (qn_ref[...] * X_DH ** -0.5)).astype(BF16)
    outs = []
    for h in range(X_HEADS):
        blk = slice(h * X_DH, (h + 1) * X_DH)
        s = _dot_nt(q[:, blk], k_ref[:, blk])
        p = jnp.exp(s - jnp.max(s, axis=-1, keepdims=True))
        l = jnp.sum(p, axis=-1, keepdims=True)
        outs.append((_dot(p.astype(BF16), v_ref[:, blk]) / l).astype(BF16))
    return x + _dot(jnp.concatenate(outs, axis=1), wo_ref[...])


def _ffn_update(x, g_ref, wi_ref, wo_ref, hid, th):
    h = _rms(x, g_ref[...]).astype(BF16)
    acc = x
    for c in range(hid // th):
        gate = _dot(h, wi_ref[:, c * th:(c + 1) * th])
        up = _dot(h, wi_ref[:, hid + c * th:hid + (c + 1) * th])
        acc = acc + _dot((_silu(gate) * up).astype(BF16), wo_ref[c * th:(c + 1) * th, :])
    return acc


def _post_body(ka, th, x_ref, a_ref, b_ref, k_ref, v_ref, wmix_ref, gx_ref, wq_ref, qn_ref,
               wo_ref, gf_ref, wi_ref, wf_ref, o_ref):
    x = x_ref[...] + _dot(a_ref[...], wmix_ref[:ka, :]) + _dot(b_ref[...], wmix_ref[ka:, :])
    x = _cross_update(x, gx_ref, wq_ref, qn_ref, k_ref, v_ref, wo_ref)
    o_ref[...] = _ffn_update(x, gf_ref, wi_ref, wf_ref, wf_ref.shape[0], th)


def _post_mixer(x, a, b, k, v, wmix, gx, wq, qn, wo, gf, wi, wf, batch, tm, th):
    m, d = x.shape
    ka, kb = a.shape[1], b.shape[1]
    per = (m // batch) // tm
    nm = k.shape[0] // batch
    hw = X_HEADS * X_DH
    row = lambda i: (i, 0)
    kv = pl.BlockSpec((nm, hw), lambda i: (i // per, 0))
    return _call(
        functools.partial(_post_body, ka, th), (m // tm,),
        [(x, pl.BlockSpec((tm, d), row)), (a, pl.BlockSpec((tm, ka), row)),
         (b, pl.BlockSpec((tm, kb), row)), (k, kv), (v, kv)],
        [_whole(wmix, True), _whole(gx), _whole(wq, True), _whole(qn), _whole(wo, True), _whole(gf),
         _whole(wi, True), _whole(wf, True)],
        pl.BlockSpec((tm, d), row), jax.ShapeDtypeStruct((m, d), F32), ("parallel",), "post_mixer")


def _row(v):
    return v.reshape(1, -1).astype(F32)


def _block_diag(w):
    n, d, e = w.shape
    eye = jnp.eye(n, dtype=w.dtype)
    return (eye[:, None, :, None] * w[:, :, None, :]).reshape(n * d, n * e)


EV_QKV = 3 * A_HEADS * A_D
EV_Z = A_HEADS * A_D


def _even_w_in(w_in):
    o = EV_QKV + EV_Z
    w_bd = w_in[..., o:o + 2 * A_HEADS]
    w_bd = jnp.pad(w_bd, ((0, 0), (0, 0), (0, LANES - 2 * A_HEADS)))
    return jnp.concatenate([w_in[..., :o], w_in[..., o + 2 * A_HEADS:], w_bd], axis=-1).astype(BF16)


def _even_layer(x, batch, tm, tb, norm_g, w, conv_qkv, a_log, dt_bias, o_norm, conv_b_w, conv_b_b,
                gate_a_w, gate_a_b, gate_x_w, gate_x_b, lru_l):
    qkv, z, xb, gb, bd = _norm_proj(x, norm_g, w, [EV_QKV, EV_Z, B_WIDTH, B_WIDTH, LANES],
                                    [F32] * 5, tm)
    pad = jnp.zeros((A_HEADS,), F32)
    avec = _row(jnp.pad(jnp.concatenate([pad, a_log.astype(F32)]), (0, LANES - 2 * A_HEADS)))
    dvec = _row(jnp.pad(jnp.concatenate([pad, dt_bias.astype(F32)]), (0, LANES - 2 * A_HEADS)))
    y_a = _delta_rule(qkv, z, bd, conv_qkv.astype(F32), avec, dvec, _row(o_norm), batch, tb)
    y_b = _rglru(xb, gb, conv_b_w.astype(F32), _row(conv_b_b), _block_diag(gate_a_w).astype(BF16),
                 _row(gate_a_b), _block_diag(gate_x_w).astype(BF16), _row(gate_x_b), _row(lru_l),
                 batch, tb)
    return y_a, y_b


def _head_pad_cols(w, heads, real, total):
    lead = w.shape[:-1]
    w = w.reshape(lead + (heads, real))
    pad = [(0, 0)] * (w.ndim - 1) + [(0, total - real)]
    return jnp.pad(w, pad).reshape(lead + (heads * total,))


def _odd_weights(w_in, w_uq, w_ukv):
    hw = C_HEADS * 2 * C_DK
    cut = 3 * hw + Q_LORA + KV_LORA
    k_rope_w = jnp.pad(w_in[..., cut:], ((0, 0), (0, 0), (D_NOPE, LANES - D_QK)))
    w = jnp.concatenate([w_in[..., :cut], k_rope_w], axis=-1).astype(BF16)
    wuq = _head_pad_cols(w_uq, D_HEADS, D_QK, LANES).astype(BF16)
    kvr = w_ukv.reshape(w_ukv.shape[:-1] + (D_HEADS, D_NOPE + LANES))
    flat = lambda a: a.reshape(a.shape[:-2] + (-1,))
    wukv = jnp.concatenate([_head_pad_cols(flat(kvr[..., :D_NOPE]), D_HEADS, D_NOPE, LANES),
                            flat(kvr[..., D_NOPE:])], axis=-1).astype(BF16)
    return w, wuq, wukv


def _odd_layer(x, pos_col, pos_row, batch, tm, tq, layer_idx, norm_g, w, c_q_norm, c_k_norm,
               lam_q1, lam_k1, lam_q2, lam_k2, c_sub_norm, q_lat_norm, wuq, kv_lat_norm, wukv,
               d_q_norm, d_k_norm):
    tile = lambda v, n: _row(jnp.tile(v.astype(F32), n))
    padn = lambda v: jnp.pad(v.astype(F32), (0, LANES - D_QK))
    half = D_ROPE // 2
    inv_freq = ROPE_THETA ** (-jnp.arange(half, dtype=F32) / half)
    freq = _row(jnp.concatenate([jnp.zeros((D_NOPE,), F32), inv_freq, inv_freq,
                                 jnp.zeros((LANES - D_QK,), F32)]))
    qc, kc, vc, qd, kd, vd = _odd_in(
        x, pos_col, norm_g, w, tile(c_q_norm, 2 * C_HEADS), tile(c_k_norm, 2 * C_HEADS),
        _row(q_lat_norm), wuq, _row(kv_lat_norm), wukv, tile(padn(d_q_norm), D_HEADS),
        tile(padn(d_k_norm), D_HEADS), freq, tm, (min(TK_DIFF, tm), min(TK_MLA, tm)))
    lam_init = 0.8 - 0.6 * math.exp(-0.3 * layer_idx)
    lamv = jnp.stack([lam_q1, lam_k1, lam_q2, lam_k2]).astype(F32)
    nb_d = NB_MLA if batch % NB_MLA == 0 else 1
    yc = _flash(qc, kc, vc, batch, tq, 1, 2, (lamv, _row(c_sub_norm), pos_col, pos_row), lam_init)
    yd = _flash(qd, kd, vd, batch, tq, nb_d, 1)
    return yc, yd


def kernel(x, mem, positions, norm_mix, norm_x, norm_mem, x_wq, x_wkv, x_q_norm, x_k_norm, x_wo, norm_ffn, ffn_w_in, ffn_w_out, ev_w_in, ev_conv_qkv, ev_a_log, ev_dt_bias, ev_o_norm, ev_conv_b_w, ev_conv_b_b, ev_gate_a_w, ev_gate_a_b, ev_gate_x_w, ev_gate_x_b, ev_lru_l, ev_w_out, od_w_in, od_c_q_norm, od_c_k_norm, od_lam_q1, od_lam_k1, od_lam_q2, od_lam_k2, od_c_sub_norm, od_q_lat_norm, od_w_uq, od_kv_lat_norm, od_w_ukv, od_d_q_norm, od_d_k_norm, od_w_out):
    batch, seq, d = x.shape
    depth = norm_mix.shape[0]
    m = batch * seq
    tm = min(512, seq)
    tb = min(256, seq)
    tq = min(T_Q, seq)
    th = ffn_w_out.shape[1]
    th = 1408 if th % 1408 == 0 else th
    xf = x.reshape(m, d).astype(F32)
    mem2 = mem.reshape(-1, d).astype(F32)
    pos_col = positions.reshape(m, 1).astype(jnp.int32)
    pos_row = positions.reshape(batch, seq // tq, 1, tq).astype(jnp.int32)
    vec = lambda a: a.astype(F32).reshape(a.shape[0], 1, -1)
    n_x, n_mem, n_ffn, n_mix = vec(norm_x), vec(norm_mem), vec(norm_ffn), vec(norm_mix)
    ev_w = _even_w_in(ev_w_in)
    od_w, od_wuq, od_wukv = _odd_weights(od_w_in, od_w_uq, od_w_ukv)
    qn_all = vec(jnp.tile(x_q_norm, (1, X_HEADS)))
    kn_all = vec(jnp.tile(x_k_norm, (1, X_HEADS)))
    wq_all, wkv_all, wo_all = x_wq.astype(BF16), x_wkv.astype(BF16), x_wo.astype(BF16)
    wi_all, wf_all = ffn_w_in.astype(BF16), ffn_w_out.astype(BF16)
    wmix_all = (ev_w_out.astype(BF16), od_w_out.astype(BF16))
    for l in range(depth):
        i = l // 2
        if l % 2 == 0:
            y_a, y_b = _even_layer(xf, batch, tm, tb, _Layer(n_mix, l), _Layer(ev_w, i), ev_conv_qkv[i], ev_a_log[i],
                                   ev_dt_bias[i], ev_o_norm[i], ev_conv_b_w[i], ev_conv_b_b[i],
                                   ev_gate_a_w[i], ev_gate_a_b[i], ev_gate_x_w[i], ev_gate_x_b[i],
                                   ev_lru_l[i])
        else:
            y_a, y_b = _odd_layer(xf, pos_col, pos_row, batch, tm, tq, l, _Layer(n_mix, l), _Layer(od_w, i),
                                  od_c_q_norm[i], od_c_k_norm[i], od_lam_q1[i], od_lam_k1[i], od_lam_q2[i],
                                  od_lam_k2[i], od_c_sub_norm[i], od_q_lat_norm[i], _Layer(od_wuq, i),
                                  od_kv_lat_norm[i], _Layer(od_wukv, i), od_d_q_norm[i], od_d_k_norm[i])
        k_mem, v_mem = _mem_kv(mem2, _Layer(n_mem, l), _Layer(wkv_all, l), _Layer(kn_all, l), batch)
        xf = _post_mixer(xf, y_a, y_b, k_mem, v_mem, _Layer(wmix_all[l % 2], i), _Layer(n_x, l),
                         _Layer(wq_all, l), _Layer(qn_all, l), _Layer(wo_all, l), _Layer(n_ffn, l),
                         _Layer(wi_all, l), _Layer(wf_all, l), batch, tm, th)
    return xf.reshape(batch, seq, d).astype(x.dtype)
```

```python
import functools
import math
from typing import NamedTuple

import jax
import jax.numpy as jnp
import numpy as np
from jax import lax
from jax.experimental import pallas as pl
from jax.experimental.pallas import tpu as pltpu

F32 = jnp.float32
BF16 = jnp.bfloat16

NORM_EPS = 1e-6
NEG_INF = -1e30
CHUNK = 64
LANES = 128
SUBLANES = 8
LRU_C = 8.0
ROPE_THETA = 10000.0
VMEM_LIMIT = 56 * 1024 * 1024

A_HEADS = 4
A_D = 128
B_WIDTH = 512
B_BLOCKS = 8
C_HEADS = 4
C_DK = 64
D_HEADS = 4
D_NOPE = 64
D_ROPE = 32
D_QK = D_NOPE + D_ROPE
Q_LORA = 256
KV_LORA = 128
X_HEADS = 4
X_DH = 128


def _params(*sem):
    return pltpu.CompilerParams(dimension_semantics=sem, vmem_limit_bytes=VMEM_LIMIT)


def _rms(x, g):
    ms = jnp.mean(x * x, axis=-1, keepdims=True)
    return x * lax.rsqrt(ms + NORM_EPS) * g


def _dot(a, b):
    return jnp.dot(a, b, preferred_element_type=F32)


def _dot_nt(a, b):
    return lax.dot_general(a, b, (((1,), (1,)), ((), ())), preferred_element_type=F32)


def _sigmoid(x):
    return 1.0 / (1.0 + jnp.exp(-x))


def _silu(x):
    return x * _sigmoid(x)


def _softplus(x):
    return jnp.maximum(x, 0.0) + jnp.log(1.0 + jnp.exp(-jnp.abs(x)))


def _full(shape):
    return pl.BlockSpec(shape, lambda *_: (0,) * len(shape))


class _Layer(NamedTuple):
    arr: jax.Array
    idx: int


def _whole(a, resident=False):
    kw = dict(pipeline_mode=pl.Buffered(1)) if resident else {}
    if isinstance(a, _Layer):
        tail = a.arr.shape[1:]
        return a.arr, pl.BlockSpec((None,) + tail, lambda *_: (a.idx,) + (0,) * len(tail), **kw)
    return a, pl.BlockSpec(a.shape, lambda *_: (0,) * a.ndim, **kw)


def _call(body, grid, tiled, whole, out_specs, out_shape, sem, name, scratch=()):
    ops = list(tiled) + list(whole)
    return pl.pallas_call(
        body, grid=grid, in_specs=[s for _, s in ops], out_specs=out_specs, out_shape=out_shape,
        scratch_shapes=list(scratch), compiler_params=_params(*sem), name=name,
    )(*[a for a, _ in ops])


def _norm_proj_body(splits, x_ref, g_ref, w_ref, *out_refs):
    h = _rms(x_ref[...], g_ref[...]).astype(BF16)
    for (start, width), o_ref in zip(splits, out_refs):
        o_ref[...] = _dot(h, w_ref[:, start:start + width]).astype(o_ref.dtype)


def _norm_proj(x, g, w, widths, dtypes, tm):
    m, d = x.shape
    starts = np.concatenate([[0], np.cumsum(widths)[:-1]]).tolist()
    splits = tuple(zip(starts, widths))
    return _call(
        functools.partial(_norm_proj_body, splits), (m // tm,),
        [(x, pl.BlockSpec((tm, d), lambda i: (i, 0)))], [_whole(g), _whole(w)],
        [pl.BlockSpec((tm, wd), lambda i: (i, 0)) for wd in widths],
        [jax.ShapeDtypeStruct((m, wd), dt) for wd, dt in zip(widths, dtypes)],
        ("parallel",), "norm_proj")


HALO = 8


def _causal_conv(x, w_ref, buf_ref, hist_ref, first):
    tb = x.shape[0]
    k = w_ref.shape[0]

    @pl.when(first)
    def _():
        hist_ref[...] = jnp.zeros_like(hist_ref)

    buf_ref[0:HALO, :] = hist_ref[...]
    buf_ref[HALO:, :] = x
    hist_ref[...] = x[tb - HALO:, :]
    full = buf_ref[...]
    y = x * w_ref[k - 1:k, :]
    for j in range(k - 1):
        y = y + pltpu.roll(full, k - 1 - j, 0)[HALO:, :] * w_ref[j:j + 1, :]
    return y


def _delta_body(tb, qkv_ref, z_ref, bd_ref, cw_ref, avec_ref, dvec_ref, onorm_ref, tri_ref,
                y_ref, buf_ref, hist_ref, state_ref):
    first = pl.program_id(1) == 0
    nh, d = A_HEADS, A_D
    qk_w = nh * d

    @pl.when(first)
    def _():
        state_ref[...] = jnp.zeros_like(state_ref)

    act = _silu(_causal_conv(qkv_ref[...], cw_ref, buf_ref, hist_ref, first))

    bd = bd_ref[...]
    beta_all = _sigmoid(bd)
    g_all = -jnp.exp(avec_ref[...]) * _softplus(bd + dvec_ref[...])
    cum_all = jnp.dot(tri_ref[...], g_all, precision=lax.Precision.HIGHEST,
                      preferred_element_type=F32)

    cum_t = cum_all.T

    ri = lax.broadcasted_iota(jnp.int32, (CHUNK, CHUNK), 0)
    ci = lax.broadcasted_iota(jnp.int32, (CHUNK, CHUNK), 1)
    incl = ri >= ci
    strict = ri > ci
    eye = jnp.where(ri == ci, 1.0, 0.0).astype(F32)
    scale = d ** -0.5

    nc = tb // CHUNK
    probs = [(h, c) for c in range(nc) for h in range(nh)]
    qs, ks, vs = {}, {}, {}
    for h in range(nh):
        q_h = act[:, h * d:(h + 1) * d]
        k_h = act[:, qk_w + h * d:qk_w + (h + 1) * d]
        q_h = q_h * (lax.rsqrt(jnp.sum(q_h * q_h, axis=-1, keepdims=True) + NORM_EPS) * scale)
        k_h = k_h * lax.rsqrt(jnp.sum(k_h * k_h, axis=-1, keepdims=True) + NORM_EPS)
        for c in range(nc):
            rows = slice(c * CHUNK, (c + 1) * CHUNK)
            qs[h, c] = q_h[rows]
            ks[h, c] = k_h[rows]
            vs[h, c] = act[rows, 2 * qk_w + h * d:2 * qk_w + (h + 1) * d]

    a_mat, rhs_wu, q_dec, k_dec_t, qk, neg_m, last = {}, {}, {}, {}, {}, {}, {}
    for pr in probs:
        h, c = pr
        rows = slice(c * CHUNK, (c + 1) * CHUNK)
        q, k, v = qs[pr], ks[pr], vs[pr]
        beta = beta_all[rows, h:h + 1]
        cum = cum_all[rows, nh + h:nh + h + 1]
        cum_row = cum_t[nh + h:nh + h + 1, c * CHUNK:(c + 1) * CHUNK]
        cum_last = cum[CHUNK - 1:CHUNK, :]
        decay = jnp.where(incl, jnp.exp(jnp.where(incl, cum - cum_row, 0.0)), 0.0)
        e_cum = jnp.exp(cum)
        kb = k * beta
        a_mat[pr] = _dot_nt(jnp.concatenate([q, kb], axis=0).astype(BF16), k.astype(BF16))
        qk[pr] = jnp.where(incl, a_mat[pr][:CHUNK] * decay, 0.0).astype(BF16)
        neg_m[pr] = jnp.where(strict, -a_mat[pr][CHUNK:] * decay, 0.0)
        rhs_wu[pr] = jnp.concatenate([kb * e_cum, v * beta], axis=1).astype(BF16)
        q_dec[pr] = q * e_cum
        k_dec_t[pr] = (k * jnp.exp(cum_last - cum)).T.astype(BF16)
        last[pr] = jnp.exp(cum_last)
    pw = dict(neg_m)
    t_inv = {pr: eye + neg_m[pr] for pr in probs}
    for _ in range(CHUNK.bit_length() - 2):
        for pr in probs:
            p_bf = pw[pr].astype(BF16)
            pw[pr] = _dot(p_bf, p_bf)
        for pr in probs:
            t_inv[pr] = _dot(t_inv[pr].astype(BF16), (eye + pw[pr]).astype(BF16))
    wu = {pr: _dot(t_inv[pr].astype(BF16), rhs_wu[pr]) for pr in probs}

    state = [state_ref[h] for h in range(nh)]
    outs = [[] for _ in range(nh)]
    for c in range(nc):
        ws = [_dot(jnp.concatenate([wu[h, c][:, :d], q_dec[h, c]], axis=0).astype(BF16),
                   state[h].astype(BF16)) for h in range(nh)]
        v_new = [(wu[h, c][:, d:] - ws[h][:CHUNK]).astype(BF16) for h in range(nh)]
        for h in range(nh):
            outs[h].append(ws[h][CHUNK:] + _dot(qk[h, c], v_new[h]))
        state = [state[h] * last[h, c] + _dot(k_dec_t[h, c], v_new[h]) for h in range(nh)]
    for h in range(nh):
        state_ref[h] = state[h]
        o = jnp.concatenate(outs[h], axis=0)
        o = _rms(o, onorm_ref[...]) * _silu(z_ref[:, h * d:(h + 1) * d])
        y_ref[:, h * d:(h + 1) * d] = o.astype(y_ref.dtype)


def _delta_rule(qkv, z, bd, conv_w, avec, dvec, onorm, batch, tb):
    m, c3 = qkv.shape
    s = m // batch
    nsb = s // tb
    cid = np.arange(tb) // CHUNK
    tri = jnp.asarray(((cid[:, None] == cid[None, :]) &
                       (np.arange(tb)[:, None] >= np.arange(tb)[None, :])).astype(np.float32))
    vw = A_HEADS * A_D
    row = lambda b, j: (b * nsb + j, 0)
    return pl.pallas_call(
        functools.partial(_delta_body, tb),
        grid=(batch, nsb),
        in_specs=[pl.BlockSpec((tb, c3), row), pl.BlockSpec((tb, vw), row),
                  pl.BlockSpec((tb, LANES), row), _full(conv_w.shape), _full(avec.shape),
                  _full(dvec.shape), _full(onorm.shape), _full(tri.shape)],
        out_specs=pl.BlockSpec((tb, vw), row),
        out_shape=jax.ShapeDtypeStruct((m, vw), BF16),
        scratch_shapes=[pltpu.VMEM((HALO + tb, c3), F32), pltpu.VMEM((HALO, c3), F32),
                        pltpu.VMEM((A_HEADS, A_D, A_D), F32)],
        compiler_params=_params("parallel", "arbitrary"),
        name="delta_rule",
    )(qkv, z, bd, conv_w, avec, dvec, onorm, tri)


def _gelu_tanh(x):
    return 0.5 * x * (1.0 + jnp.tanh(math.sqrt(2.0 / math.pi) * (x + 0.044715 * (x * x * x))))


def _lru_body(tb, xb_ref, gb_ref, cw_ref, cb_ref, wa_ref, ba_ref, wx_ref, bx_ref, l_ref,
              y_ref, buf_ref, hist_ref, h_ref):
    first = pl.program_id(1) == 0

    @pl.when(first)
    def _():
        h_ref[...] = jnp.zeros_like(h_ref)

    xc = _causal_conv(xb_ref[...], cw_ref, buf_ref, hist_ref, first) + cb_ref[...]
    xc_bf = xc.astype(BF16)
    r = _sigmoid(_dot(xc_bf, wa_ref[...]) + ba_ref[...])
    i = _sigmoid(_dot(xc_bf, wx_ref[...]) + bx_ref[...])
    log_a = (-LRU_C) * r * _softplus(-l_ref[...])
    a = jnp.exp(log_a)
    th = jnp.tanh(log_a)
    u = jnp.sqrt(-2.0 * th / (1.0 - th)) * (i * xc)
    wdt = a.shape[1]
    a3 = a.reshape(tb // SUBLANES, SUBLANES, wdt)
    u3 = u.reshape(tb // SUBLANES, SUBLANES, wdt)
    sub = lax.broadcasted_iota(jnp.int32, a3.shape, 1)
    shift = 1
    while shift < SUBLANES:
        keep = sub >= shift
        a_prev = jnp.where(keep, pltpu.roll(a3, shift, 1), 1.0)
        u_prev = jnp.where(keep, pltpu.roll(u3, shift, 1), 0.0)
        u3 = u3 + a3 * u_prev
        a3 = a3 * a_prev
        shift *= 2
    carry = h_ref[...]
    groups = []
    for r in range(tb // SUBLANES):
        h_r = u3[r] + a3[r] * carry
        carry = h_r[SUBLANES - 1:SUBLANES, :]
        groups.append(h_r)
    h_ref[...] = carry
    hs = jnp.concatenate(groups, axis=0)
    y_ref[...] = (_gelu_tanh(gb_ref[...]) * hs).astype(y_ref.dtype)


def _rglru(xb, gb, cw, cb, wa, ba, wx, bx, lru_l, batch, tb):
    m, wdt = xb.shape
    nsb = (m // batch) // tb
    row = lambda b, j: (b * nsb + j, 0)
    return pl.pallas_call(
        functools.partial(_lru_body, tb),
        grid=(batch, nsb),
        in_specs=[pl.BlockSpec((tb, wdt), row), pl.BlockSpec((tb, wdt), row),
                  _full(cw.shape), _full(cb.shape), _full(wa.shape), _full(ba.shape),
                  _full(wx.shape), _full(bx.shape), _full(lru_l.shape)],
        out_specs=pl.BlockSpec((tb, wdt), row),
        out_shape=jax.ShapeDtypeStruct((m, wdt), BF16),
        scratch_shapes=[pltpu.VMEM((HALO + tb, wdt), F32), pltpu.VMEM((HALO, wdt), F32),
                        pltpu.VMEM((1, wdt), F32)],
        compiler_params=_params("parallel", "arbitrary"),
        name="rglru",
    )(xb, gb, cw, cb, wa, ba, wx, bx, lru_l)


def _head_norm(x, nblk, denom):
    outs = []
    for h in range(nblk):
        xh = x[:, h * LANES:(h + 1) * LANES]
        ss = jnp.sum(xh * xh, axis=-1, keepdims=True)
        outs.append(xh * lax.rsqrt(ss * (1.0 / denom) + NORM_EPS))
    return jnp.concatenate(outs, axis=1)


def _half_norm(x, nblk):
    lane = lax.broadcasted_iota(jnp.int32, (1, LANES), 1)
    lo = lane < C_DK
    outs = []
    for h in range(nblk):
        xh = x[:, h * LANES:(h + 1) * LANES]
        sq = xh * xh
        s_lo = jnp.sum(jnp.where(lo, sq, 0.0), axis=-1, keepdims=True)
        s_hi = jnp.sum(jnp.where(lo, 0.0, sq), axis=-1, keepdims=True)
        inv = jnp.where(lo, lax.rsqrt(s_lo * (1.0 / C_DK) + NORM_EPS),
                        lax.rsqrt(s_hi * (1.0 / C_DK) + NORM_EPS))
        outs.append(xh * inv)
    return jnp.concatenate(outs, axis=1)


def _rope_blocks(x, nblk, cos, sin):
    lane = lax.broadcasted_iota(jnp.int32, (1, LANES), 1)
    first_half = lane < D_NOPE + D_ROPE // 2
    outs = []
    for h in range(nblk):
        xh = x[:, h * LANES:(h + 1) * LANES]
        partner = jnp.where(first_half, pltpu.roll(xh, LANES - D_ROPE // 2, 1),
                            pltpu.roll(xh, D_ROPE // 2, 1))
        outs.append(xh * cos + partner * sin)
    return jnp.concatenate(outs, axis=1)


def _store_vt(vt_ref, v, t):
    vt = v.T
    for j in range(v.shape[0] // t):
        vt_ref[j] = vt[:, j * t:(j + 1) * t].astype(vt_ref.dtype)


def _odd_in_body(tks, x_ref, pos_ref, g_ref, w_ref, cqn_ref, ckn_ref, qln_ref, wuq_ref, kvn_ref,
                 wukv_ref, dqn_ref, dkn_ref, freq_ref,
                 qc_ref, kc_ref, vc_ref, qd_ref, kd_ref, vd_ref):
    hw = C_HEADS * LANES
    h = _rms(x_ref[...], g_ref[...]).astype(BF16)
    qc = _dot(h, w_ref[:, 0:hw])
    qc_ref[...] = (_half_norm(qc, C_HEADS) * (cqn_ref[...] * (C_DK ** -0.5 * LOG2E))).astype(BF16)
    kc = _dot(h, w_ref[:, hw:2 * hw])
    kc_ref[...] = (_half_norm(kc, C_HEADS) * ckn_ref[...]).astype(BF16)
    _store_vt(vc_ref, _dot(h, w_ref[:, 2 * hw:3 * hw]), tks[0])

    lane = lax.broadcasted_iota(jnp.int32, (1, LANES), 1)
    ang = pos_ref[...].astype(F32) * freq_ref[...]
    is_rope = (lane >= D_NOPE) & (lane < D_QK)
    cos = jnp.where(is_rope, jnp.cos(ang), 1.0)
    sin_raw = jnp.sin(ang)
    sin = jnp.where(is_rope, jnp.where(lane < D_NOPE + D_ROPE // 2, -sin_raw, sin_raw), 0.0)

    off = 3 * hw
    q_lat = _dot(h, w_ref[:, off:off + Q_LORA])
    qd = _dot(_rms(q_lat, qln_ref[...]).astype(BF16), wuq_ref[...])
    qd = _head_norm(qd, D_HEADS, D_QK) * (dqn_ref[...] * (D_QK ** -0.5 * LOG2E))
    qd_ref[...] = _rope_blocks(qd, D_HEADS, cos, sin).astype(BF16)

    off += Q_LORA
    kv_lat = _dot(h, w_ref[:, off:off + KV_LORA])
    k_rope = _dot(h, w_ref[:, off + KV_LORA:off + KV_LORA + LANES])
    kvd = _dot(_rms(kv_lat, kvn_ref[...]).astype(BF16), wukv_ref[...])
    dhw = D_HEADS * LANES
    kd = kvd[:, :dhw] + jnp.concatenate([k_rope] * D_HEADS, axis=1)
    kd = _head_norm(kd, D_HEADS, D_QK) * dkn_ref[...]
    kd_ref[...] = _rope_blocks(kd, D_HEADS, cos, sin).astype(BF16)
    _store_vt(vd_ref, kvd[:, dhw:], tks[1])


def _odd_in(x, pos, g, w, cqn, ckn, qln, wuq, kvn, wukv, dqn, dkn, freq, tm, tks):
    m, d = x.shape
    hw = C_HEADS * LANES
    row = lambda i: (i, 0)
    small = [g, w, cqn, ckn, qln, wuq, kvn, wukv, dqn, dkn, freq]
    rspec = pl.BlockSpec((tm, hw), row)
    tspec = [pl.BlockSpec((tm // t, hw, t), lambda i: (i, 0, 0)) for t in tks]
    rshape = jax.ShapeDtypeStruct((m, hw), BF16)
    tshape = [jax.ShapeDtypeStruct((m // t, hw, t), BF16) for t in tks]
    return _call(
        functools.partial(_odd_in_body, tks), (m // tm,),
        [(x, pl.BlockSpec((tm, d), row)), (pos, pl.BlockSpec((tm, 1), row))],
        [_whole(a) for a in small],
        [rspec, rspec, tspec[0], rspec, rspec, tspec[1]],
        [rshape, rshape, tshape[0], rshape, rshape, tshape[1]],
        ("parallel",), "odd_in")


LOG2E = math.log2(math.e)
T_Q = 256
TK_DIFF = 256
TK_MLA = 512
SUM_ROWS = 16
NB_MLA = 2
TM_POST = 1024
TH_FFN = 256


def _flash_body(n_maps, alibi, t, nb, lam_init, *refs):
    if alibi:
        lam_ref, sub_ref, q_ref, k_ref, vt_ref, pq_ref, pk_ref, o_ref = refs
    else:
        q_ref, k_ref, vt_ref, o_ref = refs
    seq = q_ref.shape[0] // nb
    nq = seq // t
    nh = q_ref.shape[1] // LANES
    tk = vt_ref.shape[2]
    ng = seq // tk
    lane = lax.broadcasted_iota(jnp.int32, (1, LANES), 1)
    key_i = lax.broadcasted_iota(jnp.int32, (tk, t), 0)
    qry_i = lax.broadcasted_iota(jnp.int32, (tk, t), 1)
    shift = CHUNK.bit_length() - 1
    chains = [(b, h, mp) for b in range(nb) for h in range(nh) for mp in range(n_maps)]
    ones_rows = jnp.ones((SUM_ROWS, tk), BF16)
    hcols = lambda h: slice(h * LANES, (h + 1) * LANES)

    def q_block(qi, _):
        qm = {}
        for b in range(nb):
            rows = pl.ds(pl.multiple_of(b * seq + qi * t, t), t)
            for h in range(nh):
                q = q_ref[rows, hcols(h)]
                for mp in range(n_maps):
                    qm[b, h, mp] = (jnp.where((lane < C_DK) == (mp == 0), q, jnp.zeros_like(q))
                                    if n_maps == 2 else q)

        def group(g, carry, masked):
            krows = [pl.ds(pl.multiple_of(b * seq + g * tk, tk), tk) for b in range(nb)]
            s = {ch: _dot_nt(k_ref[krows[ch[0]], hcols(ch[1])], qm[ch]) for ch in chains}
            if alibi:
                for b in range(nb):
                    dist = jnp.abs(pk_ref[krows[b], :].astype(F32) - pq_ref[b, qi].astype(F32))
                    for h in range(nh):
                        bias = (2.0 ** (-2 * (h + 1)) * LOG2E) * dist
                        for mp in range(n_maps):
                            s[b, h, mp] = s[b, h, mp] - bias
            if masked:
                allowed = ((key_i + g * tk) >> shift) <= ((qry_i + qi * t) >> shift)
                for ch in chains:
                    s[ch] = jnp.where(allowed, s[ch], NEG_INF)
            vt_ext = {(b, h): jnp.concatenate([vt_ref[b * ng + g, hcols(h), :], ones_rows], axis=0)
                      for b in range(nb) for h in range(nh)}
            new = []
            for ch, (m_i, acc) in zip(chains, carry):
                m_new = jnp.maximum(m_i, jnp.max(s[ch], axis=0, keepdims=True))
                alpha = jnp.exp2(m_i - m_new)
                p = jnp.exp2((s[ch] - m_new).astype(BF16))
                new.append((m_new, alpha * acc + _dot(vt_ext[ch[:2]], p)))
            return tuple(new)

        init = tuple((jnp.full((1, t), NEG_INF, F32), jnp.zeros((LANES + SUM_ROWS, t), F32))
                     for _ in chains)
        g_last = (qi * t) // tk
        carry = lax.fori_loop(0, g_last, lambda g, c: group(g, c, False), init)
        carry = group(g_last, carry, True)
        outs = {ch: acc[:LANES] / acc[LANES:LANES + 1] for ch, (_, acc) in zip(chains, carry)}
        if n_maps == 2:
            lv = lam_ref[...]
            lam = (jnp.exp(jnp.sum(lv[0:1] * lv[1:2], axis=-1, keepdims=True))
                   - jnp.exp(jnp.sum(lv[2:3] * lv[3:4], axis=-1, keepdims=True)) + lam_init)
        for b in range(nb):
            rows = pl.ds(pl.multiple_of(b * seq + qi * t, t), t)
            for h in range(nh):
                if n_maps == 2:
                    o = (outs[b, h, 0] - lam * outs[b, h, 1]).T
                    o = _rms(o, sub_ref[...]) * (1.0 - lam_init)
                else:
                    o = outs[b, h, 0].T
                o_ref[rows, hcols(h)] = o.astype(o_ref.dtype)
        return 0

    lax.fori_loop(0, nq, q_block, 0)


def _flash(q, k, vt, batch, t, nb, n_maps, extras=None, lam_init=0.0):
    m, hw = q.shape
    s = m // batch
    nq = s // t
    tk = vt.shape[2]
    alibi = extras is not None
    spec = pl.BlockSpec((nb * s, hw), lambda b: (b, 0))
    in_specs = [spec, spec, pl.BlockSpec((nb * s // tk, hw, tk), lambda b: (b, 0, 0))]
    args = [q, k, vt]
    if alibi:
        lamv, subn, pos_col, pos_row = extras
        in_specs = [_full(lamv.shape), _full(subn.shape)] + in_specs + [
            pl.BlockSpec((nb, nq, 1, t), lambda b: (b, 0, 0, 0)),
            pl.BlockSpec((nb * s, 1), lambda b: (b, 0))]
        args = [lamv, subn] + args + [pos_row, pos_col]
    return pl.pallas_call(
        functools.partial(_flash_body, n_maps, alibi, t, nb, lam_init),
        grid=(batch // nb,),
        in_specs=in_specs,
        out_specs=spec,
        out_shape=jax.ShapeDtypeStruct((m, hw), BF16),
        compiler_params=_params("parallel"),
        name="flash_diff" if alibi else "flash_mla",
    )(*args)


def _mem_kv_body(mem_ref, g_ref, w_ref, kn_ref, k_ref, v_ref):
    hw = X_HEADS * X_DH
    mn = _rms(mem_ref[...], g_ref[...]).astype(BF16)
    k = _dot(mn, w_ref[:, :hw])
    k_ref[...] = (_head_norm(k, X_HEADS, X_DH) * kn_ref[...]).astype(BF16)
    v_ref[...] = _dot(mn, w_ref[:, hw:]).astype(BF16)


def _mem_kv(mem2, g, w, kn, batch):
    rows, d = mem2.shape
    nm = rows // batch
    hw = X_HEADS * X_DH
    return _call(
        _mem_kv_body, (batch,),
        [(mem2, pl.BlockSpec((nm, d), lambda b: (b, 0)))],
        [_whole(g), _whole(w), _whole(kn)],
        [pl.BlockSpec((nm, hw), lambda b: (b, 0))] * 2,
        [jax.ShapeDtypeStruct((rows, hw), BF16)] * 2,
        ("parallel",), "mem_kv")


def _cross_update(x, g_ref, wq_ref, qn_ref, k_ref, v_ref, wo_ref):
    q = _dot(_rms(x, g_ref[...]).astype(BF16), wq_ref[...])
    q = (_head_norm(q, X_HEADS, X_DH) * (qn_ref[...] * X_DH ** -0.5)).astype(BF16)
    outs = []
    for h in range(X_HEADS):
        blk = slice(h * X_DH, (h + 1) * X_DH)
        s = _dot_nt(q[:, blk], k_ref[:, blk])
        p = jnp.exp(s - jnp.max(s, axis=-1, keepdims=True))
        l = jnp.sum(p, axis=-1, keepdims=True)
        outs.append((_dot(p.astype(BF16), v_ref[:, blk]) / l).astype(BF16))
    return x + _dot(jnp.concatenate(outs, axis=1), wo_ref[...])


def _ffn_update(x, g_ref, wi_ref, wo_ref, hid, th):
    h = _rms(x, g_ref[...]).astype(BF16)
    acc = x
    for c in range(hid // th):
        gate = _dot(h, wi_ref[:, c * th:(c + 1) * th])
        up = _dot(h, wi_ref[:, hid + c * th:hid + (c + 1) * th])
        acc = acc + _dot((_silu(gate) * up).astype(BF16), wo_ref[c * th:(c + 1) * th, :])
    return acc


def _post_body(ka, th, x_ref, a_ref, b_ref, k_ref, v_ref, wmix_ref, gx_ref, wq_ref, qn_ref,
               wo_ref, gf_ref, wi_ref, wf_ref, o_ref):
    x = x_ref[...] + _dot(a_ref[...], wmix_ref[:ka, :]) + _dot(b_ref[...], wmix_ref[ka:, :])
    x = _cross_update(x, gx_ref, wq_ref, qn_ref, k_ref, v_ref, wo_ref)
    o_ref[...] = _ffn_update(x, gf_ref, wi_ref, wf_ref, wf_ref.shape[0], th)


def _post_mixer(x, a, b, k, v, wmix, gx, wq, qn, wo, gf, wi, wf, batch, tm, th):
    m, d = x.shape
    ka, kb = a.shape[1], b.shape[1]
    per = (m // batch) // tm
    nm = k.shape[0] // batch
    hw = X_HEADS * X_DH
    row = lambda i: (i, 0)
    kv = pl.BlockSpec((nm, hw), lambda i: (i // per, 0))
    return _call(
        functools.partial(_post_body, ka, th), (m // tm,),
        [(x, pl.BlockSpec((tm, d), row)), (a, pl.BlockSpec((tm, ka), row)),
         (b, pl.BlockSpec((tm, kb), row)), (k, kv), (v, kv)],
        [_whole(wmix, True), _whole(gx), _whole(wq, True), _whole(qn), _whole(wo, True), _whole(gf),
         _whole(wi, True), _whole(wf, True)],
        pl.BlockSpec((tm, d), row), jax.ShapeDtypeStruct((m, d), F32), ("parallel",), "post_mixer")


def _row(v):
    return v.reshape(1, -1).astype(F32)


def _block_diag(w):
    n, d, e = w.shape
    eye = jnp.eye(n, dtype=w.dtype)
    return (eye[:, None, :, None] * w[:, :, None, :]).reshape(n * d, n * e)


EV_QKV = 3 * A_HEADS * A_D
EV_Z = A_HEADS * A_D


def _even_w_in(w_in):
    o = EV_QKV + EV_Z
    w_bd = w_in[..., o:o + 2 * A_HEADS]
    w_bd = jnp.pad(w_bd, ((0, 0), (0, 0), (0, LANES - 2 * A_HEADS)))
    return jnp.concatenate([w_in[..., :o], w_in[..., o + 2 * A_HEADS:], w_bd], axis=-1).astype(BF16)


def _even_layer(x, batch, tm, tb, norm_g, w, conv_qkv, a_log, dt_bias, o_norm, conv_b_w, conv_b_b,
                gate_a_w, gate_a_b, gate_x_w, gate_x_b, lru_l):
    qkv, z, xb, gb, bd = _norm_proj(x, norm_g, w, [EV_QKV, EV_Z, B_WIDTH, B_WIDTH, LANES],
                                    [F32] * 5, tm)
    pad = jnp.zeros((A_HEADS,), F32)
    avec = _row(jnp.pad(jnp.concatenate([pad, a_log.astype(F32)]), (0, LANES - 2 * A_HEADS)))
    dvec = _row(jnp.pad(jnp.concatenate([pad, dt_bias.astype(F32)]), (0, LANES - 2 * A_HEADS)))
    y_a = _delta_rule(qkv, z, bd, conv_qkv.astype(F32), avec, dvec, _row(o_norm), batch, tb)
    y_b = _rglru(xb, gb, conv_b_w.astype(F32), _row(conv_b_b), _block_diag(gate_a_w).astype(BF16),
                 _row(gate_a_b), _block_diag(gate_x_w).astype(BF16), _row(gate_x_b), _row(lru_l),
                 batch, tb)
    return y_a, y_b


def _head_pad_cols(w, heads, real, total):
    lead = w.shape[:-1]
    w = w.reshape(lead + (heads, real))
    pad = [(0, 0)] * (w.ndim - 1) + [(0, total - real)]
    return jnp.pad(w, pad).reshape(lead + (heads * total,))


def _odd_weights(w_in, w_uq, w_ukv):
    hw = C_HEADS * 2 * C_DK
    cut = 3 * hw + Q_LORA + KV_LORA
    k_rope_w = jnp.pad(w_in[..., cut:], ((0, 0), (0, 0), (D_NOPE, LANES - D_QK)))
    w = jnp.concatenate([w_in[..., :cut], k_rope_w], axis=-1).astype(BF16)
    wuq = _head_pad_cols(w_uq, D_HEADS, D_QK, LANES).astype(BF16)
    kvr = w_ukv.reshape(w_ukv.shape[:-1] + (D_HEADS, D_NOPE + LANES))
    flat = lambda a: a.reshape(a.shape[:-2] + (-1,))
    wukv = jnp.concatenate([_head_pad_cols(flat(kvr[..., :D_NOPE]), D_HEADS, D_NOPE, LANES),
                            flat(kvr[..., D_NOPE:])], axis=-1).astype(BF16)
    return w, wuq, wukv


def _odd_layer(x, pos_col, pos_row, batch, tm, tq, layer_idx, norm_g, w, c_q_norm, c_k_norm,
               lam_q1, lam_k1, lam_q2, lam_k2, c_sub_norm, q_lat_norm, wuq, kv_lat_norm, wukv,
               d_q_norm, d_k_norm):
    tile = lambda v, n: _row(jnp.tile(v.astype(F32), n))
    padn = lambda v: jnp.pad(v.astype(F32), (0, LANES - D_QK))
    half = D_ROPE // 2
    inv_freq = ROPE_THETA ** (-jnp.arange(half, dtype=F32) / half)
    freq = _row(jnp.concatenate([jnp.zeros((D_NOPE,), F32), inv_freq, inv_freq,
                                 jnp.zeros((LANES - D_QK,), F32)]))
    qc, kc, vc, qd, kd, vd = _odd_in(
        x, pos_col, norm_g, w, tile(c_q_norm, 2 * C_HEADS), tile(c_k_norm, 2 * C_HEADS),
        _row(q_lat_norm), wuq, _row(kv_lat_norm), wukv, tile(padn(d_q_norm), D_HEADS),
        tile(padn(d_k_norm), D_HEADS), freq, tm, (min(TK_DIFF, tm), min(TK_MLA, tm)))
    lam_init = 0.8 - 0.6 * math.exp(-0.3 * layer_idx)
    lamv = jnp.stack([lam_q1, lam_k1, lam_q2, lam_k2]).astype(F32)
    nb_d = NB_MLA if batch % NB_MLA == 0 else 1
    yc = _flash(qc, kc, vc, batch, tq, nb_d, 2, (lamv, _row(c_sub_norm), pos_col, pos_row), lam_init)
    yd = _flash(qd, kd, vd, batch, tq, nb_d, 1)
    return yc, yd


def kernel(x, mem, positions, norm_mix, norm_x, norm_mem, x_wq, x_wkv, x_q_norm, x_k_norm, x_wo, norm_ffn, ffn_w_in, ffn_w_out, ev_w_in, ev_conv_qkv, ev_a_log, ev_dt_bias, ev_o_norm, ev_conv_b_w, ev_conv_b_b, ev_gate_a_w, ev_gate_a_b, ev_gate_x_w, ev_gate_x_b, ev_lru_l, ev_w_out, od_w_in, od_c_q_norm, od_c_k_norm, od_lam_q1, od_lam_k1, od_lam_q2, od_lam_k2, od_c_sub_norm, od_q_lat_norm, od_w_uq, od_kv_lat_norm, od_w_ukv, od_d_q_norm, od_d_k_norm, od_w_out):
    batch, seq, d = x.shape
    depth = norm_mix.shape[0]
    m = batch * seq
    tm = min(512, seq)
    tb = min(256, seq)
    tq = min(T_Q, seq)
    hid = ffn_w_out.shape[1]
    th = TH_FFN if hid % TH_FFN == 0 else hid
    tm_post = min(TM_POST, seq)
    xf = x.reshape(m, d).astype(F32)
    mem2 = mem.reshape(-1, d).astype(F32)
    pos_col = positions.reshape(m, 1).astype(jnp.int32)
    pos_row = positions.reshape(batch, seq // tq, 1, tq).astype(jnp.int32)
    vec = lambda a: a.astype(F32).reshape(a.shape[0], 1, -1)
    n_x, n_mem, n_ffn, n_mix = vec(norm_x), vec(norm_mem), vec(norm_ffn), vec(norm_mix)
    ev_w = _even_w_in(ev_w_in)
    od_w, od_wuq, od_wukv = _odd_weights(od_w_in, od_w_uq, od_w_ukv)
    qn_all = vec(jnp.tile(x_q_norm, (1, X_HEADS)))
    kn_all = vec(jnp.tile(x_k_norm, (1, X_HEADS)))
    wq_all, wkv_all, wo_all = x_wq.astype(BF16), x_wkv.astype(BF16), x_wo.astype(BF16)
    wi_all, wf_all = ffn_w_in.astype(BF16), ffn_w_out.astype(BF16)
    wmix_all = (ev_w_out.astype(BF16), od_w_out.astype(BF16))
    for l in range(depth):
        i = l // 2
        if l % 2 == 0:
            y_a, y_b = _even_layer(xf, batch, tm, tb, _Layer(n_mix, l), _Layer(ev_w, i), ev_conv_qkv[i], ev_a_log[i],
                                   ev_dt_bias[i], ev_o_norm[i], ev_conv_b_w[i], ev_conv_b_b[i],
                                   ev_gate_a_w[i], ev_gate_a_b[i], ev_gate_x_w[i], ev_gate_x_b[i],
                                   ev_lru_l[i])
        else:
            y_a, y_b = _odd_layer(xf, pos_col, pos_row, batch, tm, tq, l, _Layer(n_mix, l), _Layer(od_w, i),
                                  od_c_q_norm[i], od_c_k_norm[i], od_lam_q1[i], od_lam_k1[i], od_lam_q2[i],
                                  od_lam_k2[i], od_c_sub_norm[i], od_q_lat_norm[i], _Layer(od_wuq, i),
                                  od_kv_lat_norm[i], _Layer(od_wukv, i), od_d_q_norm[i], od_d_k_norm[i])
        k_mem, v_mem = _mem_kv(mem2, _Layer(n_mem, l), _Layer(wkv_all, l), _Layer(kn_all, l), batch)
        xf = _post_mixer(xf, y_a, y_b, k_mem, v_mem, _Layer(wmix_all[l % 2], i), _Layer(n_x, l),
                         _Layer(wq_all, l), _Layer(qn_all, l), _Layer(wo_all, l), _Layer(n_ffn, l),
                         _Layer(wi_all, l), _Layer(wf_all, l), batch, tm_post, th)
    return xf.reshape(batch, seq, d).astype(x.dtype)
```

```python
import functools
import math
from typing import NamedTuple

import jax
import jax.numpy as jnp
import numpy as np
from jax import lax
from jax.experimental import pallas as pl
from jax.experimental.pallas import tpu as pltpu

F32 = jnp.float32
BF16 = jnp.bfloat16

NORM_EPS = 1e-6
NEG_INF = -1e30
CHUNK = 64
LANES = 128
SUBLANES = 8
LRU_C = 8.0
ROPE_THETA = 10000.0
VMEM_LIMIT = 56 * 1024 * 1024

A_HEADS = 4
A_D = 128
B_WIDTH = 512
B_BLOCKS = 8
C_HEADS = 4
C_DK = 64
D_HEADS = 4
D_NOPE = 64
D_ROPE = 32
D_QK = D_NOPE + D_ROPE
Q_LORA = 256
KV_LORA = 128
X_HEADS = 4
X_DH = 128


def _params(*sem):
    return pltpu.CompilerParams(dimension_semantics=sem, vmem_limit_bytes=VMEM_LIMIT)


def _rms(x, g):
    ms = jnp.mean(x * x, axis=-1, keepdims=True)
    return x * lax.rsqrt(ms + NORM_EPS) * g


def _dot(a, b):
    return jnp.dot(a, b, preferred_element_type=F32)


def _dot_nt(a, b):
    return lax.dot_general(a, b, (((1,), (1,)), ((), ())), preferred_element_type=F32)


def _sigmoid(x):
    return 1.0 / (1.0 + jnp.exp(-x))


def _silu(x):
    return x * _sigmoid(x)


def _softplus(x):
    return jnp.maximum(x, 0.0) + jnp.log(1.0 + jnp.exp(-jnp.abs(x)))


def _full(shape):
    return pl.BlockSpec(shape, lambda *_: (0,) * len(shape))


class _Layer(NamedTuple):
    arr: jax.Array
    idx: int


def _whole(a, resident=False):
    kw = dict(pipeline_mode=pl.Buffered(1)) if resident else {}
    if isinstance(a, _Layer):
        tail = a.arr.shape[1:]
        return a.arr, pl.BlockSpec((None,) + tail, lambda *_: (a.idx,) + (0,) * len(tail), **kw)
    return a, pl.BlockSpec(a.shape, lambda *_: (0,) * a.ndim, **kw)


def _call(body, grid, tiled, whole, out_specs, out_shape, sem, name, scratch=()):
    ops = list(tiled) + list(whole)
    return pl.pallas_call(
        body, grid=grid, in_specs=[s for _, s in ops], out_specs=out_specs, out_shape=out_shape,
        scratch_shapes=list(scratch), compiler_params=_params(*sem), name=name,
    )(*[a for a, _ in ops])


def _norm_proj_body(splits, x_ref, g_ref, w_ref, *out_refs):
    h = _rms(x_ref[...], g_ref[...]).astype(BF16)
    for (start, width), o_ref in zip(splits, out_refs):
        o_ref[...] = _dot(h, w_ref[:, start:start + width]).astype(o_ref.dtype)


def _norm_proj(x, g, w, widths, dtypes, tm):
    m, d = x.shape
    starts = np.concatenate([[0], np.cumsum(widths)[:-1]]).tolist()
    splits = tuple(zip(starts, widths))
    return _call(
        functools.partial(_norm_proj_body, splits), (m // tm,),
        [(x, pl.BlockSpec((tm, d), lambda i: (i, 0)))], [_whole(g), _whole(w)],
        [pl.BlockSpec((tm, wd), lambda i: (i, 0)) for wd in widths],
        [jax.ShapeDtypeStruct((m, wd), dt) for wd, dt in zip(widths, dtypes)],
        ("parallel",), "norm_proj")


HALO = 8
TB_LRU = 512


def _causal_conv(x_ref, w_ref, hist_ref, cols=slice(None)):
    x = x_ref[:, cols]
    tb = x.shape[0]
    k = w_ref.shape[0]
    full = jnp.concatenate([hist_ref[:, cols], x], axis=0)
    hist_ref[:, cols] = x[tb - HALO:, :]
    y = x * w_ref[k - 1:k, cols]
    for j in range(k - 1):
        y = y + pltpu.roll(full, k - 1 - j, 0)[HALO:, :] * w_ref[j:j + 1, cols]
    return y


def _l2_heads(x, nblk):
    outs = []
    for h in range(nblk):
        xh = x[:, h * LANES:(h + 1) * LANES]
        outs.append(xh * lax.rsqrt(jnp.sum(xh * xh, axis=-1, keepdims=True) + NORM_EPS))
    return jnp.concatenate(outs, axis=1)


def _delta_body(tb, qkv_ref, z_ref, bd_ref, cw_ref, avec_ref, dvec_ref, onorm_ref, tri_ref,
                y_ref, hist_ref, state_ref):
    @pl.when(pl.program_id(1) == 0)
    def _():
        state_ref[...] = jnp.zeros_like(state_ref)
        hist_ref[...] = jnp.zeros_like(hist_ref)

    nh, d = A_HEADS, A_D
    qk_w = nh * d

    def act(i):
        return _silu(_causal_conv(qkv_ref, cw_ref, hist_ref, slice(i * qk_w, (i + 1) * qk_w)))

    q_all = _l2_heads(act(0), nh) * d ** -0.5
    k_all = _l2_heads(act(1), nh)
    v_all = act(2)
    bd = bd_ref[...]
    beta_all = _sigmoid(bd)
    g_all = -jnp.exp(avec_ref[...]) * _softplus(bd + dvec_ref[...])
    cum_all = jnp.dot(tri_ref[...], g_all, precision=lax.Precision.HIGHEST,
                      preferred_element_type=F32)
    cum_t = cum_all.T

    ri = lax.broadcasted_iota(jnp.int32, (CHUNK, CHUNK), 0)
    ci = lax.broadcasted_iota(jnp.int32, (CHUNK, CHUNK), 1)
    incl = ri >= ci
    strict = ri > ci
    eye = jnp.where(ri == ci, 1.0, 0.0).astype(F32)

    nc = tb // CHUNK
    probs = [(h, c) for c in range(nc) for h in range(nh)]
    a_mat, rhs_wu, q_dec, k_dec_t, qk, neg_m, last = {}, {}, {}, {}, {}, {}, {}
    for pr in probs:
        h, c = pr
        rows = slice(c * CHUNK, (c + 1) * CHUNK)
        cols = slice(h * d, (h + 1) * d)
        q, k, v = q_all[rows, cols], k_all[rows, cols], v_all[rows, cols]
        beta = beta_all[rows, h:h + 1]
        cum = cum_all[rows, nh + h:nh + h + 1]
        cum_row = cum_t[nh + h:nh + h + 1, c * CHUNK:(c + 1) * CHUNK]
        cum_last = cum[CHUNK - 1:CHUNK, :]
        decay = jnp.where(incl, jnp.exp(jnp.where(incl, cum - cum_row, 0.0)), 0.0)
        e_cum = jnp.exp(cum)
        kb = k * beta
        a_mat[pr] = _dot_nt(jnp.concatenate([q, kb], axis=0).astype(BF16), k.astype(BF16))
        qk[pr] = jnp.where(incl, a_mat[pr][:CHUNK] * decay, 0.0).astype(BF16)
        neg_m[pr] = jnp.where(strict, -a_mat[pr][CHUNK:] * decay, 0.0)
        rhs_wu[pr] = jnp.concatenate([kb * e_cum, v * beta], axis=1).astype(BF16)
        q_dec[pr] = q * e_cum
        k_dec_t[pr] = (k * jnp.exp(cum_last - cum)).T.astype(BF16)
        last[pr] = jnp.exp(cum_last)
    pw = dict(neg_m)
    t_inv = {pr: eye + neg_m[pr] for pr in probs}
    for _ in range(CHUNK.bit_length() - 2):
        for pr in probs:
            p_bf = pw[pr].astype(BF16)
            pw[pr] = _dot(p_bf, p_bf)
        for pr in probs:
            t_inv[pr] = _dot(t_inv[pr].astype(BF16), (eye + pw[pr]).astype(BF16))
    wu = {pr: _dot(t_inv[pr].astype(BF16), rhs_wu[pr]) for pr in probs}

    state = [state_ref[h] for h in range(nh)]
    outs = [[] for _ in range(nh)]
    for c in range(nc):
        ws = [_dot(jnp.concatenate([wu[h, c][:, :d], q_dec[h, c]], axis=0).astype(BF16),
                   state[h].astype(BF16)) for h in range(nh)]
        v_new = [(wu[h, c][:, d:] - ws[h][:CHUNK]).astype(BF16) for h in range(nh)]
        for h in range(nh):
            outs[h].append(ws[h][CHUNK:] + _dot(qk[h, c], v_new[h]))
        state = [state[h] * last[h, c] + _dot(k_dec_t[h, c], v_new[h]) for h in range(nh)]
    for h in range(nh):
        state_ref[h] = state[h]
        o = jnp.concatenate(outs[h], axis=0)
        o = _rms(o, onorm_ref[...]) * _silu(z_ref[:, h * d:(h + 1) * d])
        y_ref[:, h * d:(h + 1) * d] = o.astype(y_ref.dtype)


def _delta_rule(qkv, z, bd, conv_w, avec, dvec, onorm, batch, tb):
    m, c3 = qkv.shape
    s = m // batch
    nsb = s // tb
    cid = np.arange(tb) // CHUNK
    tri = jnp.asarray(((cid[:, None] == cid[None, :]) &
                       (np.arange(tb)[:, None] >= np.arange(tb)[None, :])).astype(np.float32))
    vw = A_HEADS * A_D
    row = lambda b, j: (b * nsb + j, 0)
    return pl.pallas_call(
        functools.partial(_delta_body, tb),
        grid=(batch, nsb),
        in_specs=[pl.BlockSpec((tb, c3), row), pl.BlockSpec((tb, vw), row),
                  pl.BlockSpec((tb, LANES), row), _full(conv_w.shape), _full(avec.shape),
                  _full(dvec.shape), _full(onorm.shape), _full(tri.shape)],
        out_specs=pl.BlockSpec((tb, vw), row),
        out_shape=jax.ShapeDtypeStruct((m, vw), BF16),
        scratch_shapes=[pltpu.VMEM((HALO, c3), F32),
                        pltpu.VMEM((A_HEADS, A_D, A_D), F32)],
        compiler_params=_params("parallel", "arbitrary"),
        name="delta_rule",
    )(qkv, z, bd, conv_w, avec, dvec, onorm, tri)


def _gelu_tanh(x):
    return 0.5 * x * (1.0 + jnp.tanh(math.sqrt(2.0 / math.pi) * (x + 0.044715 * (x * x * x))))


def _lru_body(tb, xb_ref, gb_ref, cw_ref, cb_ref, wa_ref, ba_ref, wx_ref, bx_ref, l_ref,
              y_ref, hist_ref, h_ref):
    @pl.when(pl.program_id(1) == 0)
    def _():
        h_ref[...] = jnp.zeros_like(h_ref)
        hist_ref[...] = jnp.zeros_like(hist_ref)

    xc = _causal_conv(xb_ref, cw_ref, hist_ref) + cb_ref[...]
    xc_bf = xc.astype(BF16)
    r = _sigmoid(_dot(xc_bf, wa_ref[...]) + ba_ref[...])
    i = _sigmoid(_dot(xc_bf, wx_ref[...]) + bx_ref[...])
    log_a = (-LRU_C) * r * _softplus(-l_ref[...])
    a = jnp.exp(log_a)
    th = jnp.tanh(log_a)
    u = jnp.sqrt(-2.0 * th / (1.0 - th)) * (i * xc)
    wdt = a.shape[1]
    a3 = a.reshape(tb // SUBLANES, SUBLANES, wdt)
    u3 = u.reshape(tb // SUBLANES, SUBLANES, wdt)
    sub = lax.broadcasted_iota(jnp.int32, a3.shape, 1)
    shift = 1
    while shift < SUBLANES:
        keep = sub >= shift
        a_prev = jnp.where(keep, pltpu.roll(a3, shift, 1), 1.0)
        u_prev = jnp.where(keep, pltpu.roll(u3, shift, 1), 0.0)
        u3 = u3 + a3 * u_prev
        a3 = a3 * a_prev
        shift *= 2
    carry = h_ref[...]
    groups = []
    for r in range(tb // SUBLANES):
        h_r = u3[r] + a3[r] * carry
        carry = h_r[SUBLANES - 1:SUBLANES, :]
        groups.append(h_r)
    h_ref[...] = carry
    hs = jnp.concatenate(groups, axis=0)
    y_ref[...] = (_gelu_tanh(gb_ref[...]) * hs).astype(y_ref.dtype)


def _rglru(xb, gb, cw, cb, wa, ba, wx, bx, lru_l, batch, tb):
    m, wdt = xb.shape
    nsb = (m // batch) // tb
    row = lambda b, j: (b * nsb + j, 0)
    return pl.pallas_call(
        functools.partial(_lru_body, tb),
        grid=(batch, nsb),
        in_specs=[pl.BlockSpec((tb, wdt), row), pl.BlockSpec((tb, wdt), row),
                  _full(cw.shape), _full(cb.shape), _full(wa.shape), _full(ba.shape),
                  _full(wx.shape), _full(bx.shape), _full(lru_l.shape)],
        out_specs=pl.BlockSpec((tb, wdt), row),
        out_shape=jax.ShapeDtypeStruct((m, wdt), BF16),
        scratch_shapes=[pltpu.VMEM((HALO, wdt), F32), pltpu.VMEM((1, wdt), F32)],
        compiler_params=_params("parallel", "arbitrary"),
        name="rglru",
    )(xb, gb, cw, cb, wa, ba, wx, bx, lru_l)


def _head_norm(x, nblk, denom):
    outs = []
    for h in range(nblk):
        xh = x[:, h * LANES:(h + 1) * LANES]
        ss = jnp.sum(xh * xh, axis=-1, keepdims=True)
        outs.append(xh * lax.rsqrt(ss * (1.0 / denom) + NORM_EPS))
    return jnp.concatenate(outs, axis=1)


def _group_norm(x, ones_ref, denom):
    ss = _dot((x * x).astype(BF16), ones_ref[...])
    return x * lax.rsqrt(ss * (1.0 / denom) + NORM_EPS)


def _group_ones(width, group):
    gid = np.arange(width) // group
    return jnp.asarray(gid[:, None] == gid[None, :], dtype=BF16)


def _rope_table_body(pos_ref, freq_ref, cos_ref, sin_ref):
    lane = lax.broadcasted_iota(jnp.int32, (1, LANES), 1)
    ang = pos_ref[...].astype(F32) * freq_ref[...]
    is_rope = (lane >= D_NOPE) & (lane < D_QK)
    cos_ref[...] = jnp.where(is_rope, jnp.cos(ang), 1.0)
    sin_raw = jnp.sin(ang)
    sin_ref[...] = jnp.where(is_rope, jnp.where(lane < D_NOPE + D_ROPE // 2, -sin_raw, sin_raw), 0.0)


def _rope_tables(pos, tm):
    m = pos.shape[0]
    half = D_ROPE // 2
    inv_freq = ROPE_THETA ** (-jnp.arange(half, dtype=F32) / half)
    freq = jnp.concatenate([jnp.zeros((D_NOPE,), F32), inv_freq, inv_freq,
                            jnp.zeros((LANES - D_QK,), F32)]).reshape(1, LANES)
    row = lambda i: (i, 0)
    return _call(
        _rope_table_body, (m // tm,), [(pos, pl.BlockSpec((tm, 1), row))], [_whole(freq)],
        [pl.BlockSpec((tm, LANES), row)] * 2, [jax.ShapeDtypeStruct((m, LANES), F32)] * 2,
        ("parallel",), "rope_tables")


def _rope_blocks(x, nblk, cos, sin):
    lane = lax.broadcasted_iota(jnp.int32, (1, LANES), 1)
    first_half = lane < D_NOPE + D_ROPE // 2
    outs = []
    for h in range(nblk):
        xh = x[:, h * LANES:(h + 1) * LANES]
        partner = jnp.where(first_half, pltpu.roll(xh, LANES - D_ROPE // 2, 1),
                            pltpu.roll(xh, D_ROPE // 2, 1))
        outs.append(xh * cos + partner * sin)
    return jnp.concatenate(outs, axis=1)


def _store_vt(vt_ref, v, t):
    vt = v.T
    for j in range(v.shape[0] // t):
        vt_ref[j] = vt[:, j * t:(j + 1) * t].astype(vt_ref.dtype)


def _odd_in_body(tks, x_ref, cos_ref, sin_ref, g_ref, w_ref, cqn_ref, ckn_ref, qln_ref, wuq_ref, kvn_ref,
                 wukv_ref, dqn_ref, dkn_ref, ones_c_ref, ones_d_ref,
                 qc_ref, kc_ref, vc_ref, qd_ref, kd_ref, vd_ref):
    hw = C_HEADS * LANES
    h = _rms(x_ref[...], g_ref[...]).astype(BF16)
    qc = _dot(h, w_ref[:, 0:hw])
    qc_ref[...] = (_group_norm(qc, ones_c_ref, C_DK) * (cqn_ref[...] * (C_DK ** -0.5 * LOG2E))).astype(BF16)
    kc = _dot(h, w_ref[:, hw:2 * hw])
    kc_ref[...] = (_group_norm(kc, ones_c_ref, C_DK) * ckn_ref[...]).astype(BF16)
    _store_vt(vc_ref, _dot(h, w_ref[:, 2 * hw:3 * hw]), tks[0])

    cos = cos_ref[...]
    sin = sin_ref[...]
    off = 3 * hw
    q_lat = _dot(h, w_ref[:, off:off + Q_LORA])
    qd = _dot(_rms(q_lat, qln_ref[...]).astype(BF16), wuq_ref[...])
    qd = _group_norm(qd, ones_d_ref, D_QK) * (dqn_ref[...] * (D_QK ** -0.5 * LOG2E))
    qd_ref[...] = _rope_blocks(qd, D_HEADS, cos, sin).astype(BF16)

    off += Q_LORA
    kv_lat = _dot(h, w_ref[:, off:off + KV_LORA])
    k_rope = _dot(h, w_ref[:, off + KV_LORA:off + KV_LORA + LANES])
    kvd = _dot(_rms(kv_lat, kvn_ref[...]).astype(BF16), wukv_ref[...])
    dhw = D_HEADS * LANES
    kd = kvd[:, :dhw] + jnp.concatenate([k_rope] * D_HEADS, axis=1)
    kd = _group_norm(kd, ones_d_ref, D_QK) * dkn_ref[...]
    kd_ref[...] = _rope_blocks(kd, D_HEADS, cos, sin).astype(BF16)
    _store_vt(vd_ref, kvd[:, dhw:], tks[1])


def _odd_in(x, cos, sin, g, w, cqn, ckn, qln, wuq, kvn, wukv, dqn, dkn, tm, tks):
    m, d = x.shape
    hw = C_HEADS * LANES
    row = lambda i: (i, 0)
    small = [g, w, cqn, ckn, qln, wuq, kvn, wukv, dqn, dkn, _group_ones(hw, C_DK), _group_ones(hw, LANES)]
    rspec = pl.BlockSpec((tm, hw), row)
    tspec = [pl.BlockSpec((tm // t, hw, t), lambda i: (i, 0, 0)) for t in tks]
    rshape = jax.ShapeDtypeStruct((m, hw), BF16)
    tshape = [jax.ShapeDtypeStruct((m // t, hw, t), BF16) for t in tks]
    tab = pl.BlockSpec((tm, LANES), row)
    return _call(
        functools.partial(_odd_in_body, tks), (m // tm,),
        [(x, pl.BlockSpec((tm, d), row)), (cos, tab), (sin, tab)],
        [_whole(a) for a in small],
        [rspec, rspec, tspec[0], rspec, rspec, tspec[1]],
        [rshape, rshape, tshape[0], rshape, rshape, tshape[1]],
        ("parallel",), "odd_in")


LOG2E = math.log2(math.e)
T_Q = 256
TK_DIFF = 256
TK_MLA = 512
SUM_ROWS = 16
NB_MLA = 2
TM_POST = 1024
TH_FFN = 256


def _flash_body(n_maps, alibi, t, nb, lam_init, *refs):
    if alibi:
        lam_ref, sub_ref, q_ref, k_ref, vt_ref, pq_ref, pk_ref, o_ref = refs
    else:
        q_ref, k_ref, vt_ref, o_ref = refs
    seq = q_ref.shape[0] // nb
    nq = seq // t
    nh = q_ref.shape[1] // LANES
    tk = vt_ref.shape[2]
    ng = seq // tk
    lane = lax.broadcasted_iota(jnp.int32, (1, LANES), 1)
    key_i = lax.broadcasted_iota(jnp.int32, (tk, t), 0)
    qry_i = lax.broadcasted_iota(jnp.int32, (tk, t), 1)
    shift = CHUNK.bit_length() - 1
    chains = [(b, h, mp) for b in range(nb) for h in range(nh) for mp in range(n_maps)]
    ones_rows = jnp.ones((SUM_ROWS, tk), BF16)
    hcols = lambda h: slice(h * LANES, (h + 1) * LANES)

    def q_block(qi, _):
        qm = {}
        for b in range(nb):
            rows = pl.ds(pl.multiple_of(b * seq + qi * t, t), t)
            for h in range(nh):
                q = q_ref[rows, hcols(h)]
                for mp in range(n_maps):
                    qm[b, h, mp] = (jnp.where((lane < C_DK) == (mp == 0), q, jnp.zeros_like(q))
                                    if n_maps == 2 else q)

        def group(g, carry, masked):
            krows = [pl.ds(pl.multiple_of(b * seq + g * tk, tk), tk) for b in range(nb)]
            s = {ch: _dot_nt(k_ref[krows[ch[0]], hcols(ch[1])], qm[ch]) for ch in chains}
            if alibi:
                for b in range(nb):
                    dist = jnp.abs(pk_ref[krows[b], :].astype(F32) - pq_ref[b, qi].astype(F32))
                    for h in range(nh):
                        bias = (2.0 ** (-2 * (h + 1)) * LOG2E) * dist
                        for mp in range(n_maps):
                            s[b, h, mp] = s[b, h, mp] - bias
            if masked:
                allowed = ((key_i + g * tk) >> shift) <= ((qry_i + qi * t) >> shift)
                for ch in chains:
                    s[ch] = jnp.where(allowed, s[ch], NEG_INF)
            vt_ext = {(b, h): jnp.concatenate([vt_ref[b * ng + g, hcols(h), :], ones_rows], axis=0)
                      for b in range(nb) for h in range(nh)}
            new = []
            for ch, (m_i, acc) in zip(chains, carry):
                m_new = jnp.maximum(m_i, jnp.max(s[ch], axis=0, keepdims=True))
                alpha = jnp.exp2(m_i - m_new)
                p = jnp.exp2((s[ch] - m_new).astype(BF16))
                new.append((m_new, alpha * acc + _dot(vt_ext[ch[:2]], p)))
            return tuple(new)

        init = tuple((jnp.full((1, t), NEG_INF, F32), jnp.zeros((LANES + SUM_ROWS, t), F32))
                     for _ in chains)
        g_last = (qi * t) // tk
        carry = lax.fori_loop(0, g_last, lambda g, c: group(g, c, False), init)
        carry = group(g_last, carry, True)
        outs = {ch: acc[:LANES] / acc[LANES:LANES + 1] for ch, (_, acc) in zip(chains, carry)}
        if n_maps == 2:
            lv = lam_ref[...]
            lam = (jnp.exp(jnp.sum(lv[0:1] * lv[1:2], axis=-1, keepdims=True))
                   - jnp.exp(jnp.sum(lv[2:3] * lv[3:4], axis=-1, keepdims=True)) + lam_init)
        for b in range(nb):
            rows = pl.ds(pl.multiple_of(b * seq + qi * t, t), t)
            for h in range(nh):
                if n_maps == 2:
                    o = (outs[b, h, 0] - lam * outs[b, h, 1]).T
                    o = _rms(o, sub_ref[...]) * (1.0 - lam_init)
                else:
                    o = outs[b, h, 0].T
                o_ref[rows, hcols(h)] = o.astype(o_ref.dtype)
        return 0

    lax.fori_loop(0, nq, q_block, 0)


def _flash(q, k, vt, batch, t, nb, n_maps, extras=None, lam_init=0.0):
    m, hw = q.shape
    s = m // batch
    nq = s // t
    tk = vt.shape[2]
    alibi = extras is not None
    spec = pl.BlockSpec((nb * s, hw), lambda b: (b, 0))
    in_specs = [spec, spec, pl.BlockSpec((nb * s // tk, hw, tk), lambda b: (b, 0, 0))]
    args = [q, k, vt]
    if alibi:
        lamv, subn, pos_col, pos_row = extras
        in_specs = [_full(lamv.shape), _full(subn.shape)] + in_specs + [
            pl.BlockSpec((nb, nq, 1, t), lambda b: (b, 0, 0, 0)),
            pl.BlockSpec((nb * s, 1), lambda b: (b, 0))]
        args = [lamv, subn] + args + [pos_row, pos_col]
    return pl.pallas_call(
        functools.partial(_flash_body, n_maps, alibi, t, nb, lam_init),
        grid=(batch // nb,),
        in_specs=in_specs,
        out_specs=spec,
        out_shape=jax.ShapeDtypeStruct((m, hw), BF16),
        compiler_params=_params("parallel"),
        name="flash_diff" if alibi else "flash_mla",
    )(*args)


def _mem_kv_body(mem_ref, g_ref, w_ref, kn_ref, k_ref, v_ref):
    hw = X_HEADS * X_DH
    mn = _rms(mem_ref[...], g_ref[...]).astype(BF16)
    k = _dot(mn, w_ref[:, :hw])
    k_ref[...] = (_head_norm(k, X_HEADS, X_DH) * kn_ref[...]).astype(BF16)
    v_ref[...] = _dot(mn, w_ref[:, hw:]).astype(BF16)


def _mem_kv(mem2, g, w, kn, batch):
    rows, d = mem2.shape
    nm = rows // batch
    hw = X_HEADS * X_DH
    return _call(
        _mem_kv_body, (batch,),
        [(mem2, pl.BlockSpec((nm, d), lambda b: (b, 0)))],
        [_whole(g), _whole(w), _whole(kn)],
        [pl.BlockSpec((nm, hw), lambda b: (b, 0))] * 2,
        [jax.ShapeDtypeStruct((rows, hw), BF16)] * 2,
        ("parallel",), "mem_kv")


def _cross_update(x, g_ref, wq_ref, qn_ref, k_ref, v_ref, wo_ref):
    q = _dot(_rms(x, g_ref[...]).astype(BF16), wq_ref[...])
    q = (_head_norm(q, X_HEADS, X_DH) * (qn_ref[...] * X_DH ** -0.5)).astype(BF16)
    outs = []
    for h in range(X_HEADS):
        blk = slice(h * X_DH, (h + 1) * X_DH)
        s = _dot_nt(q[:, blk], k_ref[:, blk])
        p = jnp.exp(s - jnp.max(s, axis=-1, keepdims=True))
        l = jnp.sum(p, axis=-1, keepdims=True)
        outs.append((_dot(p.astype(BF16), v_ref[:, blk]) / l).astype(BF16))
    return x + _dot(jnp.concatenate(outs, axis=1), wo_ref[...])


def _ffn_update(x, g_ref, wi_ref, wo_ref, hid, th):
    h = _rms(x, g_ref[...]).astype(BF16)
    acc = x
    for c in range(hid // th):
        gate = _dot(h, wi_ref[:, c * th:(c + 1) * th])
        up = _dot(h, wi_ref[:, hid + c * th:hid + (c + 1) * th])
        acc = acc + _dot((_silu(gate) * up).astype(BF16), wo_ref[c * th:(c + 1) * th, :])
    return acc


def _post_body(ka, th, x_ref, a_ref, b_ref, k_ref, v_ref, wmix_ref, gx_ref, wq_ref, qn_ref,
               wo_ref, gf_ref, wi_ref, wf_ref, o_ref):
    x = x_ref[...] + _dot(a_ref[...], wmix_ref[:ka, :]) + _dot(b_ref[...], wmix_ref[ka:, :])
    x = _cross_update(x, gx_ref, wq_ref, qn_ref, k_ref, v_ref, wo_ref)
    o_ref[...] = _ffn_update(x, gf_ref, wi_ref, wf_ref, wf_ref.shape[0], th)


def _post_mixer(x, a, b, k, v, wmix, gx, wq, qn, wo, gf, wi, wf, batch, tm, th):
    m, d = x.shape
    ka, kb = a.shape[1], b.shape[1]
    per = (m // batch) // tm
    nm = k.shape[0] // batch
    hw = X_HEADS * X_DH
    row = lambda i: (i, 0)
    kv = pl.BlockSpec((nm, hw), lambda i: (i // per, 0))
    return _call(
        functools.partial(_post_body, ka, th), (m // tm,),
        [(x, pl.BlockSpec((tm, d), row)), (a, pl.BlockSpec((tm, ka), row)),
         (b, pl.BlockSpec((tm, kb), row)), (k, kv), (v, kv)],
        [_whole(wmix, True), _whole(gx), _whole(wq, True), _whole(qn), _whole(wo, True), _whole(gf),
         _whole(wi, True), _whole(wf, True)],
        pl.BlockSpec((tm, d), row), jax.ShapeDtypeStruct((m, d), F32), ("parallel",), "post_mixer")


def _row(v):
    return v.reshape(1, -1).astype(F32)


def _block_diag(w):
    n, d, e = w.shape
    eye = jnp.eye(n, dtype=w.dtype)
    return (eye[:, None, :, None] * w[:, :, None, :]).reshape(n * d, n * e)


EV_QKV = 3 * A_HEADS * A_D
EV_Z = A_HEADS * A_D


def _even_w_in(w_in):
    o = EV_QKV + EV_Z
    w_bd = w_in[..., o:o + 2 * A_HEADS]
    w_bd = jnp.pad(w_bd, ((0, 0), (0, 0), (0, LANES - 2 * A_HEADS)))
    return jnp.concatenate([w_in[..., :o], w_in[..., o + 2 * A_HEADS:], w_bd], axis=-1).astype(BF16)


def _even_layer(x, batch, tm, tb, norm_g, w, conv_qkv, a_log, dt_bias, o_norm, conv_b_w, conv_b_b,
                gate_a_w, gate_a_b, gate_x_w, gate_x_b, lru_l):
    qkv, z, xb, gb, bd = _norm_proj(x, norm_g, w, [EV_QKV, EV_Z, B_WIDTH, B_WIDTH, LANES],
                                    [F32] * 5, tm)
    pad = jnp.zeros((A_HEADS,), F32)
    avec = _row(jnp.pad(jnp.concatenate([pad, a_log.astype(F32)]), (0, LANES - 2 * A_HEADS)))
    dvec = _row(jnp.pad(jnp.concatenate([pad, dt_bias.astype(F32)]), (0, LANES - 2 * A_HEADS)))
    y_a = _delta_rule(qkv, z, bd, conv_qkv.astype(F32), avec, dvec, _row(o_norm), batch, tb)
    y_b = _rglru(xb, gb, conv_b_w.astype(F32), _row(conv_b_b), _block_diag(gate_a_w).astype(BF16),
                 _row(gate_a_b), _block_diag(gate_x_w).astype(BF16), _row(gate_x_b), _row(lru_l),
                 batch, min(TB_LRU, tm))
    return y_a, y_b


def _head_pad_cols(w, heads, real, total):
    lead = w.shape[:-1]
    w = w.reshape(lead + (heads, real))
    pad = [(0, 0)] * (w.ndim - 1) + [(0, total - real)]
    return jnp.pad(w, pad).reshape(lead + (heads * total,))


def _odd_weights(w_in, w_uq, w_ukv):
    hw = C_HEADS * 2 * C_DK
    cut = 3 * hw + Q_LORA + KV_LORA
    k_rope_w = jnp.pad(w_in[..., cut:], ((0, 0), (0, 0), (D_NOPE, LANES - D_QK)))
    w = jnp.concatenate([w_in[..., :cut], k_rope_w], axis=-1).astype(BF16)
    wuq = _head_pad_cols(w_uq, D_HEADS, D_QK, LANES).astype(BF16)
    kvr = w_ukv.reshape(w_ukv.shape[:-1] + (D_HEADS, D_NOPE + LANES))
    flat = lambda a: a.reshape(a.shape[:-2] + (-1,))
    wukv = jnp.concatenate([_head_pad_cols(flat(kvr[..., :D_NOPE]), D_HEADS, D_NOPE, LANES),
                            flat(kvr[..., D_NOPE:])], axis=-1).astype(BF16)
    return w, wuq, wukv


def _odd_layer(x, rope, pos_col, pos_row, batch, tm, tq, layer_idx, norm_g, w, c_q_norm, c_k_norm,
               lam_q1, lam_k1, lam_q2, lam_k2, c_sub_norm, q_lat_norm, wuq, kv_lat_norm, wukv,
               d_q_norm, d_k_norm):
    tile = lambda v, n: _row(jnp.tile(v.astype(F32), n))
    padn = lambda v: jnp.pad(v.astype(F32), (0, LANES - D_QK))
    qc, kc, vc, qd, kd, vd = _odd_in(
        x, rope[0], rope[1], norm_g, w, tile(c_q_norm, 2 * C_HEADS), tile(c_k_norm, 2 * C_HEADS),
        _row(q_lat_norm), wuq, _row(kv_lat_norm), wukv, tile(padn(d_q_norm), D_HEADS),
        tile(padn(d_k_norm), D_HEADS), tm, (min(TK_DIFF, tm), min(TK_MLA, tm)))
    lam_init = 0.8 - 0.6 * math.exp(-0.3 * layer_idx)
    lamv = jnp.stack([lam_q1, lam_k1, lam_q2, lam_k2]).astype(F32)
    nb_d = NB_MLA if batch % NB_MLA == 0 else 1
    yc = _flash(qc, kc, vc, batch, tq, nb_d, 2, (lamv, _row(c_sub_norm), pos_col, pos_row), lam_init)
    yd = _flash(qd, kd, vd, batch, tq, nb_d, 1)
    return yc, yd


def kernel(x, mem, positions, norm_mix, norm_x, norm_mem, x_wq, x_wkv, x_q_norm, x_k_norm, x_wo, norm_ffn, ffn_w_in, ffn_w_out, ev_w_in, ev_conv_qkv, ev_a_log, ev_dt_bias, ev_o_norm, ev_conv_b_w, ev_conv_b_b, ev_gate_a_w, ev_gate_a_b, ev_gate_x_w, ev_gate_x_b, ev_lru_l, ev_w_out, od_w_in, od_c_q_norm, od_c_k_norm, od_lam_q1, od_lam_k1, od_lam_q2, od_lam_k2, od_c_sub_norm, od_q_lat_norm, od_w_uq, od_kv_lat_norm, od_w_ukv, od_d_q_norm, od_d_k_norm, od_w_out):
    batch, seq, d = x.shape
    depth = norm_mix.shape[0]
    m = batch * seq
    tm = min(512, seq)
    tb = min(256, seq)
    tq = min(T_Q, seq)
    hid = ffn_w_out.shape[1]
    th = TH_FFN if hid % TH_FFN == 0 else hid
    tm_post = min(TM_POST, seq)
    xf = x.reshape(m, d).astype(F32)
    mem2 = mem.reshape(-1, d).astype(F32)
    pos_col = positions.reshape(m, 1).astype(jnp.int32)
    pos_row = positions.reshape(batch, seq // tq, 1, tq).astype(jnp.int32)
    rope = _rope_tables(pos_col, tm) if depth > 1 else None
    vec = lambda a: a.astype(F32).reshape(a.shape[0], 1, -1)
    n_x, n_mem, n_ffn, n_mix = vec(norm_x), vec(norm_mem), vec(norm_ffn), vec(norm_mix)
    ev_w = _even_w_in(ev_w_in)
    od_w, od_wuq, od_wukv = _odd_weights(od_w_in, od_w_uq, od_w_ukv)
    qn_all = vec(jnp.tile(x_q_norm, (1, X_HEADS)))
    kn_all = vec(jnp.tile(x_k_norm, (1, X_HEADS)))
    wq_all, wkv_all, wo_all = x_wq.astype(BF16), x_wkv.astype(BF16), x_wo.astype(BF16)
    wi_all, wf_all = ffn_w_in.astype(BF16), ffn_w_out.astype(BF16)
    wmix_all = (ev_w_out.astype(BF16), od_w_out.astype(BF16))
    for l in range(depth):
        i = l // 2
        if l % 2 == 0:
            y_a, y_b = _even_layer(xf, batch, tm, tb, _Layer(n_mix, l), _Layer(ev_w, i), ev_conv_qkv[i], ev_a_log[i],
                                   ev_dt_bias[i], ev_o_norm[i], ev_conv_b_w[i], ev_conv_b_b[i],
                                   ev_gate_a_w[i], ev_gate_a_b[i], ev_gate_x_w[i], ev_gate_x_b[i],
                                   ev_lru_l[i])
        else:
            y_a, y_b = _odd_layer(xf, rope, pos_col, pos_row, batch, tm, tq, l, _Layer(n_mix, l), _Layer(od_w, i),
                                  od_c_q_norm[i], od_c_k_norm[i], od_lam_q1[i], od_lam_k1[i], od_lam_q2[i],
                                  od_lam_k2[i], od_c_sub_norm[i], od_q_lat_norm[i], _Layer(od_wuq, i),
                                  od_kv_lat_norm[i], _Layer(od_wukv, i), od_d_q_norm[i], od_d_k_norm[i])
        k_mem, v_mem = _mem_kv(mem2, _Layer(n_mem, l), _Layer(wkv_all, l), _Layer(kn_all, l), batch)
        xf = _post_mixer(xf, y_a, y_b, k_mem, v_mem, _Layer(wmix_all[l % 2], i), _Layer(n_x, l),
                         _Layer(wq_all, l), _Layer(qn_all, l), _Layer(wo_all, l), _Layer(n_ffn, l),
                         _Layer(wi_all, l), _Layer(wf_all, l), batch, tm_post, th)
    return xf.reshape(batch, seq, d).astype(x.dtype)
```

```python
import functools
import math
from typing import NamedTuple

import jax
import jax.numpy as jnp
import numpy as np
from jax import lax
from jax.experimental import pallas as pl
from jax.experimental.pallas import tpu as pltpu

F32 = jnp.float32
BF16 = jnp.bfloat16

NORM_EPS = 1e-6
NEG_INF = -1e30
CHUNK = 64
LANES = 128
SUBLANES = 8
LRU_C = 8.0
ROPE_THETA = 10000.0
VMEM_LIMIT = 56 * 1024 * 1024

A_HEADS = 4
A_D = 128
B_WIDTH = 512
B_BLOCKS = 8
C_HEADS = 4
C_DK = 64
D_HEADS = 4
D_NOPE = 64
D_ROPE = 32
D_QK = D_NOPE + D_ROPE
Q_LORA = 256
KV_LORA = 128
X_HEADS = 4
X_DH = 128


def _params(*sem):
    return pltpu.CompilerParams(dimension_semantics=sem, vmem_limit_bytes=VMEM_LIMIT)


def _rms(x, g):
    ms = jnp.mean(x * x, axis=-1, keepdims=True)
    return x * lax.rsqrt(ms + NORM_EPS) * g


def _dot(a, b):
    return jnp.dot(a, b, preferred_element_type=F32)


def _dot_nt(a, b):
    return lax.dot_general(a, b, (((1,), (1,)), ((), ())), preferred_element_type=F32)


def _sigmoid(x):
    return 0.5 * jnp.tanh(0.5 * x) + 0.5


def _silu(x):
    h = 0.5 * x
    return h + h * jnp.tanh(h)


def _softplus(x):
    return jnp.maximum(x, 0.0) + jnp.log(1.0 + jnp.exp(-jnp.abs(x)))


def _full(shape):
    return pl.BlockSpec(shape, lambda *_: (0,) * len(shape))


class _Layer(NamedTuple):
    arr: jax.Array
    idx: int


def _whole(a, resident=False):
    kw = dict(pipeline_mode=pl.Buffered(1)) if resident else {}
    if isinstance(a, _Layer):
        tail = a.arr.shape[1:]
        return a.arr, pl.BlockSpec((None,) + tail, lambda *_: (a.idx,) + (0,) * len(tail), **kw)
    return a, pl.BlockSpec(a.shape, lambda *_: (0,) * a.ndim, **kw)


def _call(body, grid, tiled, whole, out_specs, out_shape, sem, name, scratch=()):
    ops = list(tiled) + list(whole)
    return pl.pallas_call(
        body, grid=grid, in_specs=[s for _, s in ops], out_specs=out_specs, out_shape=out_shape,
        scratch_shapes=list(scratch), compiler_params=_params(*sem), name=name,
    )(*[a for a, _ in ops])


def _norm_proj_body(splits, x_ref, g_ref, w_ref, *out_refs):
    h = _rms(x_ref[...], g_ref[...]).astype(BF16)
    for (start, width), o_ref in zip(splits, out_refs):
        o_ref[...] = _dot(h, w_ref[:, start:start + width]).astype(o_ref.dtype)


def _norm_proj(x, g, w, widths, dtypes, tm):
    m, d = x.shape
    starts = np.concatenate([[0], np.cumsum(widths)[:-1]]).tolist()
    splits = tuple(zip(starts, widths))
    return _call(
        functools.partial(_norm_proj_body, splits), (m // tm,),
        [(x, pl.BlockSpec((tm, d), lambda i: (i, 0)))], [_whole(g), _whole(w)],
        [pl.BlockSpec((tm, wd), lambda i: (i, 0)) for wd in widths],
        [jax.ShapeDtypeStruct((m, wd), dt) for wd, dt in zip(widths, dtypes)],
        ("parallel",), "norm_proj")


HALO = 8
TB_LRU = 512


def _causal_conv(x_ref, w_ref, hist_ref, cols=slice(None)):
    x = x_ref[:, cols]
    tb = x.shape[0]
    k = w_ref.shape[0]
    full = jnp.concatenate([hist_ref[:, cols], x], axis=0)
    hist_ref[:, cols] = x[tb - HALO:, :]
    y = x * w_ref[k - 1:k, cols]
    for j in range(k - 1):
        y = y + pltpu.roll(full, k - 1 - j, 0)[HALO:, :] * w_ref[j:j + 1, cols]
    return y


def _l2_heads(x, nblk):
    outs = []
    for h in range(nblk):
        xh = x[:, h * LANES:(h + 1) * LANES]
        outs.append(xh * lax.rsqrt(jnp.sum(xh * xh, axis=-1, keepdims=True) + NORM_EPS))
    return jnp.concatenate(outs, axis=1)


def _delta_body(tb, qkv_ref, z_ref, bd_ref, cw_ref, avec_ref, dvec_ref, onorm_ref, tri_ref,
                y_ref, hist_ref, state_ref):
    @pl.when(pl.program_id(1) == 0)
    def _():
        state_ref[...] = jnp.zeros_like(state_ref)
        hist_ref[...] = jnp.zeros_like(hist_ref)

    nh, d = A_HEADS, A_D
    qk_w = nh * d

    def act(i):
        return _silu(_causal_conv(qkv_ref, cw_ref, hist_ref, slice(i * qk_w, (i + 1) * qk_w)))

    q_all = _l2_heads(act(0), nh) * d ** -0.5
    k_all = _l2_heads(act(1), nh)
    v_all = act(2)
    bd = bd_ref[...]
    beta_all = _sigmoid(bd)
    g_all = -jnp.exp(avec_ref[...]) * _softplus(bd + dvec_ref[...])
    cum_all = jnp.dot(tri_ref[...], g_all, precision=lax.Precision.HIGHEST,
                      preferred_element_type=F32)
    cum_t = cum_all.T

    ri = lax.broadcasted_iota(jnp.int32, (CHUNK, CHUNK), 0)
    ci = lax.broadcasted_iota(jnp.int32, (CHUNK, CHUNK), 1)
    incl = ri >= ci
    strict = ri > ci
    eye = jnp.where(ri == ci, 1.0, 0.0).astype(F32)

    nc = tb // CHUNK
    probs = [(h, c) for c in range(nc) for h in range(nh)]
    a_mat, rhs_wu, q_dec, k_dec_t, qk, neg_m, last = {}, {}, {}, {}, {}, {}, {}
    for pr in probs:
        h, c = pr
        rows = slice(c * CHUNK, (c + 1) * CHUNK)
        cols = slice(h * d, (h + 1) * d)
        q, k, v = q_all[rows, cols], k_all[rows, cols], v_all[rows, cols]
        beta = beta_all[rows, h:h + 1]
        cum = cum_all[rows, nh + h:nh + h + 1]
        cum_row = cum_t[nh + h:nh + h + 1, c * CHUNK:(c + 1) * CHUNK]
        cum_last = cum[CHUNK - 1:CHUNK, :]
        decay = jnp.where(incl, jnp.exp(jnp.where(incl, cum - cum_row, 0.0)), 0.0)
        e_cum = jnp.exp(cum)
        kb = k * beta
        a_mat[pr] = _dot_nt(jnp.concatenate([q, kb], axis=0).astype(BF16), k.astype(BF16))
        qk[pr] = jnp.where(incl, a_mat[pr][:CHUNK] * decay, 0.0).astype(BF16)
        neg_m[pr] = jnp.where(strict, -a_mat[pr][CHUNK:] * decay, 0.0)
        rhs_wu[pr] = jnp.concatenate([kb * e_cum, v * beta], axis=1).astype(BF16)
        q_dec[pr] = q * e_cum
        k_dec_t[pr] = (k * jnp.exp(cum_last - cum)).T.astype(BF16)
        last[pr] = jnp.exp(cum_last)
    pw = dict(neg_m)
    t_inv = {pr: eye + neg_m[pr] for pr in probs}
    for _ in range(CHUNK.bit_length() - 2):
        for pr in probs:
            p_bf = pw[pr].astype(BF16)
            pw[pr] = _dot(p_bf, p_bf)
        for pr in probs:
            t_inv[pr] = _dot(t_inv[pr].astype(BF16), (eye + pw[pr]).astype(BF16))
    wu = {pr: _dot(t_inv[pr].astype(BF16), rhs_wu[pr]) for pr in probs}

    state = [state_ref[h] for h in range(nh)]
    outs = [[] for _ in range(nh)]
    for c in range(nc):
        ws = [_dot(jnp.concatenate([wu[h, c][:, :d], q_dec[h, c]], axis=0).astype(BF16),
                   state[h].astype(BF16)) for h in range(nh)]
        v_new = [(wu[h, c][:, d:] - ws[h][:CHUNK]).astype(BF16) for h in range(nh)]
        for h in range(nh):
            outs[h].append(ws[h][CHUNK:] + _dot(qk[h, c], v_new[h]))
        state = [state[h] * last[h, c] + _dot(k_dec_t[h, c], v_new[h]) for h in range(nh)]
    for h in range(nh):
        state_ref[h] = state[h]
        o = jnp.concatenate(outs[h], axis=0)
        o = _rms(o, onorm_ref[...]) * _silu(z_ref[:, h * d:(h + 1) * d])
        y_ref[:, h * d:(h + 1) * d] = o.astype(y_ref.dtype)


def _delta_rule(qkv, z, bd, conv_w, avec, dvec, onorm, batch, tb):
    m, c3 = qkv.shape
    s = m // batch
    nsb = s // tb
    cid = np.arange(tb) // CHUNK
    tri = jnp.asarray(((cid[:, None] == cid[None, :]) &
                       (np.arange(tb)[:, None] >= np.arange(tb)[None, :])).astype(np.float32))
    vw = A_HEADS * A_D
    row = lambda b, j: (b * nsb + j, 0)
    return pl.pallas_call(
        functools.partial(_delta_body, tb),
        grid=(batch, nsb),
        in_specs=[pl.BlockSpec((tb, c3), row), pl.BlockSpec((tb, vw), row),
                  pl.BlockSpec((tb, LANES), row), _full(conv_w.shape), _full(avec.shape),
                  _full(dvec.shape), _full(onorm.shape), _full(tri.shape)],
        out_specs=pl.BlockSpec((tb, vw), row),
        out_shape=jax.ShapeDtypeStruct((m, vw), BF16),
        scratch_shapes=[pltpu.VMEM((HALO, c3), F32),
                        pltpu.VMEM((A_HEADS, A_D, A_D), F32)],
        compiler_params=_params("parallel", "arbitrary"),
        name="delta_rule",
    )(qkv, z, bd, conv_w, avec, dvec, onorm, tri)


def _gelu_tanh(x):
    return 0.5 * x * (1.0 + jnp.tanh(math.sqrt(2.0 / math.pi) * (x + 0.044715 * (x * x * x))))


def _lru_body(tb, xb_ref, gb_ref, cw_ref, cb_ref, wa_ref, ba_ref, wx_ref, bx_ref, l_ref,
              y_ref, hist_ref, h_ref):
    @pl.when(pl.program_id(1) == 0)
    def _():
        h_ref[...] = jnp.zeros_like(h_ref)
        hist_ref[...] = jnp.zeros_like(hist_ref)

    xc = _causal_conv(xb_ref, cw_ref, hist_ref) + cb_ref[...]
    xc_bf = xc.astype(BF16)
    r = _sigmoid(_dot(xc_bf, wa_ref[...]) + ba_ref[...])
    i = _sigmoid(_dot(xc_bf, wx_ref[...]) + bx_ref[...])
    log_a = (-LRU_C) * r * _softplus(-l_ref[...])
    a = jnp.exp(log_a)
    th = jnp.tanh(log_a)
    u = jnp.sqrt(-2.0 * th / (1.0 - th)) * (i * xc)
    wdt = a.shape[1]
    a3 = a.reshape(tb // SUBLANES, SUBLANES, wdt)
    u3 = u.reshape(tb // SUBLANES, SUBLANES, wdt)
    sub = lax.broadcasted_iota(jnp.int32, a3.shape, 1)
    shift = 1
    while shift < SUBLANES:
        keep = sub >= shift
        a_prev = jnp.where(keep, pltpu.roll(a3, shift, 1), 1.0)
        u_prev = jnp.where(keep, pltpu.roll(u3, shift, 1), 0.0)
        u3 = u3 + a3 * u_prev
        a3 = a3 * a_prev
        shift *= 2
    carry = h_ref[...]
    groups = []
    for r in range(tb // SUBLANES):
        h_r = u3[r] + a3[r] * carry
        carry = h_r[SUBLANES - 1:SUBLANES, :]
        groups.append(h_r)
    h_ref[...] = carry
    hs = jnp.concatenate(groups, axis=0)
    y_ref[...] = (_gelu_tanh(gb_ref[...]) * hs).astype(y_ref.dtype)


def _rglru(xb, gb, cw, cb, wa, ba, wx, bx, lru_l, batch, tb):
    m, wdt = xb.shape
    nsb = (m // batch) // tb
    row = lambda b, j: (b * nsb + j, 0)
    return pl.pallas_call(
        functools.partial(_lru_body, tb),
        grid=(batch, nsb),
        in_specs=[pl.BlockSpec((tb, wdt), row), pl.BlockSpec((tb, wdt), row),
                  _full(cw.shape), _full(cb.shape), _full(wa.shape), _full(ba.shape),
                  _full(wx.shape), _full(bx.shape), _full(lru_l.shape)],
        out_specs=pl.BlockSpec((tb, wdt), row),
        out_shape=jax.ShapeDtypeStruct((m, wdt), BF16),
        scratch_shapes=[pltpu.VMEM((HALO, wdt), F32), pltpu.VMEM((1, wdt), F32)],
        compiler_params=_params("parallel", "arbitrary"),
        name="rglru",
    )(xb, gb, cw, cb, wa, ba, wx, bx, lru_l)


def _head_norm(x, nblk, denom):
    outs = []
    for h in range(nblk):
        xh = x[:, h * LANES:(h + 1) * LANES]
        ss = jnp.sum(xh * xh, axis=-1, keepdims=True)
        outs.append(xh * lax.rsqrt(ss * (1.0 / denom) + NORM_EPS))
    return jnp.concatenate(outs, axis=1)


def _group_norm(x, ones_ref, denom):
    ss = _dot((x * x).astype(BF16), ones_ref[...])
    return x * lax.rsqrt(ss * (1.0 / denom) + NORM_EPS)


def _group_ones(width, group):
    gid = np.arange(width) // group
    return jnp.asarray(gid[:, None] == gid[None, :], dtype=BF16)


def _rope_table_body(pos_ref, freq_ref, cos_ref, sin_ref):
    lane = lax.broadcasted_iota(jnp.int32, (1, LANES), 1)
    ang = pos_ref[...].astype(F32) * freq_ref[...]
    is_rope = (lane >= D_NOPE) & (lane < D_QK)
    cos_ref[...] = jnp.where(is_rope, jnp.cos(ang), 1.0)
    sin_raw = jnp.sin(ang)
    sin_ref[...] = jnp.where(is_rope, jnp.where(lane < D_NOPE + D_ROPE // 2, -sin_raw, sin_raw), 0.0)


def _rope_tables(pos, tm):
    m = pos.shape[0]
    half = D_ROPE // 2
    inv_freq = ROPE_THETA ** (-jnp.arange(half, dtype=F32) / half)
    freq = jnp.concatenate([jnp.zeros((D_NOPE,), F32), inv_freq, inv_freq,
                            jnp.zeros((LANES - D_QK,), F32)]).reshape(1, LANES)
    row = lambda i: (i, 0)
    return _call(
        _rope_table_body, (m // tm,), [(pos, pl.BlockSpec((tm, 1), row))], [_whole(freq)],
        [pl.BlockSpec((tm, LANES), row)] * 2, [jax.ShapeDtypeStruct((m, LANES), F32)] * 2,
        ("parallel",), "rope_tables")


def _rope_blocks(x, nblk, cos, sin):
    lane = lax.broadcasted_iota(jnp.int32, (1, LANES), 1)
    first_half = lane < D_NOPE + D_ROPE // 2
    outs = []
    for h in range(nblk):
        xh = x[:, h * LANES:(h + 1) * LANES]
        partner = jnp.where(first_half, pltpu.roll(xh, LANES - D_ROPE // 2, 1),
                            pltpu.roll(xh, D_ROPE // 2, 1))
        outs.append(xh * cos + partner * sin)
    return jnp.concatenate(outs, axis=1)


def _store_vt(vt_ref, v, t):
    vt = v.T
    for j in range(v.shape[0] // t):
        vt_ref[j] = vt[:, j * t:(j + 1) * t].astype(vt_ref.dtype)


def _odd_in_body(tks, x_ref, cos_ref, sin_ref, g_ref, w_ref, cqn_ref, ckn_ref, qln_ref, wuq_ref, kvn_ref,
                 wukv_ref, dqn_ref, dkn_ref, ones_c_ref, ones_d_ref,
                 qc_ref, kc_ref, vc_ref, qd_ref, kd_ref, vd_ref):
    hw = C_HEADS * LANES
    h = _rms(x_ref[...], g_ref[...]).astype(BF16)
    qc = _dot(h, w_ref[:, 0:hw])
    qc_ref[...] = (_group_norm(qc, ones_c_ref, C_DK) * (cqn_ref[...] * (C_DK ** -0.5 * LOG2E))).astype(BF16)
    kc = _dot(h, w_ref[:, hw:2 * hw])
    kc_ref[...] = (_group_norm(kc, ones_c_ref, C_DK) * ckn_ref[...]).astype(BF16)
    _store_vt(vc_ref, _dot(h, w_ref[:, 2 * hw:3 * hw]), tks[0])

    cos = cos_ref[...]
    sin = sin_ref[...]
    off = 3 * hw
    q_lat = _dot(h, w_ref[:, off:off + Q_LORA])
    qd = _dot(_rms(q_lat, qln_ref[...]).astype(BF16), wuq_ref[...])
    qd = _group_norm(qd, ones_d_ref, D_QK) * (dqn_ref[...] * (D_QK ** -0.5 * LOG2E))
    qd_ref[...] = _rope_blocks(qd, D_HEADS, cos, sin).astype(BF16)

    off += Q_LORA
    kv_lat = _dot(h, w_ref[:, off:off + KV_LORA])
    k_rope = _dot(h, w_ref[:, off + KV_LORA:off + KV_LORA + LANES])
    kvd = _dot(_rms(kv_lat, kvn_ref[...]).astype(BF16), wukv_ref[...])
    dhw = D_HEADS * LANES
    kd = kvd[:, :dhw] + jnp.concatenate([k_rope] * D_HEADS, axis=1)
    kd = _group_norm(kd, ones_d_ref, D_QK) * dkn_ref[...]
    kd_ref[...] = _rope_blocks(kd, D_HEADS, cos, sin).astype(BF16)
    _store_vt(vd_ref, kvd[:, dhw:], tks[1])


def _odd_in(x, cos, sin, g, w, cqn, ckn, qln, wuq, kvn, wukv, dqn, dkn, tm, tks):
    m, d = x.shape
    hw = C_HEADS * LANES
    row = lambda i: (i, 0)
    small = [g, w, cqn, ckn, qln, wuq, kvn, wukv, dqn, dkn, _group_ones(hw, C_DK), _group_ones(hw, LANES)]
    rspec = pl.BlockSpec((tm, hw), row)
    tspec = [pl.BlockSpec((tm // t, hw, t), lambda i: (i, 0, 0)) for t in tks]
    rshape = jax.ShapeDtypeStruct((m, hw), BF16)
    tshape = [jax.ShapeDtypeStruct((m // t, hw, t), BF16) for t in tks]
    tab = pl.BlockSpec((tm, LANES), row)
    return _call(
        functools.partial(_odd_in_body, tks), (m // tm,),
        [(x, pl.BlockSpec((tm, d), row)), (cos, tab), (sin, tab)],
        [_whole(a) for a in small],
        [rspec, rspec, tspec[0], rspec, rspec, tspec[1]],
        [rshape, rshape, tshape[0], rshape, rshape, tshape[1]],
        ("parallel",), "odd_in")


LOG2E = math.log2(math.e)
T_Q = 256
TK_DIFF = 256
TK_MLA = 512
SUM_ROWS = 16
QK_AHEAD = 6
NB_MLA = 2
TM_POST = 1024
TH_FFN = 256


def _flash_body(n_maps, alibi, t, nb, lam_init, *refs):
    if alibi:
        lam_ref, sub_ref, q_ref, k_ref, vt_ref, pq_ref, pk_ref, o_ref = refs
    else:
        q_ref, k_ref, vt_ref, o_ref = refs
    seq = q_ref.shape[0] // nb
    nq = seq // t
    nh = q_ref.shape[1] // LANES
    tk = vt_ref.shape[2]
    ng = seq // tk
    lane = lax.broadcasted_iota(jnp.int32, (1, LANES), 1)
    key_i = lax.broadcasted_iota(jnp.int32, (tk, t), 0)
    qry_i = lax.broadcasted_iota(jnp.int32, (tk, t), 1)
    shift = CHUNK.bit_length() - 1
    chains = [(b, h, mp) for b in range(nb) for h in range(nh) for mp in range(n_maps)]
    ones_rows = jnp.ones((SUM_ROWS, tk), BF16)
    hcols = lambda h: slice(h * LANES, (h + 1) * LANES)

    def q_block(qi, _):
        qm = {}
        for b in range(nb):
            rows = pl.ds(pl.multiple_of(b * seq + qi * t, t), t)
            for h in range(nh):
                q = q_ref[rows, hcols(h)]
                for mp in range(n_maps):
                    qm[b, h, mp] = (jnp.where((lane < C_DK) == (mp == 0), q, jnp.zeros_like(q))
                                    if n_maps == 2 else q)

        def group(g, carry, masked):
            krows = [pl.ds(pl.multiple_of(b * seq + g * tk, tk), tk) for b in range(nb)]
            if masked:
                allowed = ((key_i + g * tk) >> shift) <= ((qry_i + qi * t) >> shift)
            bias = {}

            def scores(ch):
                b, h, _ = ch
                s = _dot_nt(k_ref[krows[b], hcols(h)], qm[ch])
                if alibi:
                    if (b, h) not in bias:
                        dist = jnp.abs(pk_ref[krows[b], :].astype(F32) - pq_ref[b, qi].astype(F32))
                        bias[b, h] = (2.0 ** (-2 * (h + 1)) * LOG2E) * dist
                    s = s - bias[b, h]
                return jnp.where(allowed, s, NEG_INF) if masked else s

            s = {ch: scores(ch) for ch in chains[:QK_AHEAD]}
            new = []
            for i, (ch, (m_i, acc)) in enumerate(zip(chains, carry)):
                if i + QK_AHEAD < len(chains):
                    nxt = chains[i + QK_AHEAD]
                    s[nxt] = scores(nxt)
                b, h, _ = ch
                vt_ext = jnp.concatenate([vt_ref[b * ng + g, hcols(h), :], ones_rows], axis=0)
                s_ch = s.pop(ch)
                m_new = jnp.maximum(m_i, jnp.max(s_ch, axis=0, keepdims=True))
                alpha = jnp.exp2(m_i - m_new)
                p = jnp.exp2((s_ch - m_new).astype(BF16))
                new.append((m_new, alpha * acc + _dot(vt_ext, p)))
            return tuple(new)

        init = tuple((jnp.full((1, t), NEG_INF, F32), jnp.zeros((LANES + SUM_ROWS, t), F32))
                     for _ in chains)
        g_last = (qi * t) // tk
        carry = lax.fori_loop(0, g_last, lambda g, c: group(g, c, False), init)
        carry = group(g_last, carry, True)
        outs = {ch: acc[:LANES] / acc[LANES:LANES + 1] for ch, (_, acc) in zip(chains, carry)}
        if n_maps == 2:
            lv = lam_ref[...]
            lam = (jnp.exp(jnp.sum(lv[0:1] * lv[1:2], axis=-1, keepdims=True))
                   - jnp.exp(jnp.sum(lv[2:3] * lv[3:4], axis=-1, keepdims=True)) + lam_init)
        for b in range(nb):
            rows = pl.ds(pl.multiple_of(b * seq + qi * t, t), t)
            for h in range(nh):
                if n_maps == 2:
                    o = (outs[b, h, 0] - lam * outs[b, h, 1]).T
                    o = _rms(o, sub_ref[...]) * (1.0 - lam_init)
                else:
                    o = outs[b, h, 0].T
                o_ref[rows, hcols(h)] = o.astype(o_ref.dtype)
        return 0

    lax.fori_loop(0, nq, q_block, 0)


def _flash(q, k, vt, batch, t, nb, n_maps, extras=None, lam_init=0.0):
    m, hw = q.shape
    s = m // batch
    nq = s // t
    tk = vt.shape[2]
    alibi = extras is not None
    spec = pl.BlockSpec((nb * s, hw), lambda b: (b, 0))
    in_specs = [spec, spec, pl.BlockSpec((nb * s // tk, hw, tk), lambda b: (b, 0, 0))]
    args = [q, k, vt]
    if alibi:
        lamv, subn, pos_col, pos_row = extras
        in_specs = [_full(lamv.shape), _full(subn.shape)] + in_specs + [
            pl.BlockSpec((nb, nq, 1, t), lambda b: (b, 0, 0, 0)),
            pl.BlockSpec((nb * s, 1), lambda b: (b, 0))]
        args = [lamv, subn] + args + [pos_row, pos_col]
    return pl.pallas_call(
        functools.partial(_flash_body, n_maps, alibi, t, nb, lam_init),
        grid=(batch // nb,),
        in_specs=in_specs,
        out_specs=spec,
        out_shape=jax.ShapeDtypeStruct((m, hw), BF16),
        compiler_params=_params("parallel"),
        name="flash_diff" if alibi else "flash_mla",
    )(*args)


def _mem_kv_body(mem_ref, g_ref, w_ref, kn_ref, k_ref, v_ref):
    hw = X_HEADS * X_DH
    mn = _rms(mem_ref[...], g_ref[...]).astype(BF16)
    k = _dot(mn, w_ref[:, :hw])
    k_ref[...] = (_head_norm(k, X_HEADS, X_DH) * kn_ref[...]).astype(BF16)
    v_ref[...] = _dot(mn, w_ref[:, hw:]).astype(BF16)


def _mem_kv(mem2, g, w, kn, batch):
    rows, d = mem2.shape
    layers = w.shape[0]
    nm = rows // batch
    hw = X_HEADS * X_DH
    per_layer = lambda a: (a, pl.BlockSpec((None,) + a.shape[1:], lambda l, b: (l, 0, 0)))
    return _call(
        _mem_kv_body, (layers, batch),
        [(mem2, pl.BlockSpec((nm, d), lambda l, b: (b, 0))), per_layer(g), per_layer(w), per_layer(kn)],
        [],
        [pl.BlockSpec((None, nm, hw), lambda l, b: (l, b, 0))] * 2,
        [jax.ShapeDtypeStruct((layers, rows, hw), BF16)] * 2,
        ("parallel", "parallel"), "mem_kv")


def _cross_update(x, g_ref, wq_ref, qn_ref, k_ref, v_ref, wo_ref):
    q = _dot(_rms(x, g_ref[...]).astype(BF16), wq_ref[...])
    q = (_head_norm(q, X_HEADS, X_DH) * (qn_ref[...] * X_DH ** -0.5)).astype(BF16)
    outs = []
    for h in range(X_HEADS):
        blk = slice(h * X_DH, (h + 1) * X_DH)
        s = _dot_nt(q[:, blk], k_ref[:, blk])
        p = jnp.exp(s - jnp.max(s, axis=-1, keepdims=True))
        l = jnp.sum(p, axis=-1, keepdims=True)
        outs.append((_dot(p.astype(BF16), v_ref[:, blk]) / l).astype(BF16))
    return x + _dot(jnp.concatenate(outs, axis=1), wo_ref[...])


def _ffn_update(x, g_ref, wi_ref, wo_ref, hid, th):
    h = _rms(x, g_ref[...]).astype(BF16)
    acc = x
    for c in range(hid // th):
        gate = _dot(h, wi_ref[:, c * th:(c + 1) * th])
        up = _dot(h, wi_ref[:, hid + c * th:hid + (c + 1) * th])
        acc = acc + _dot((_silu(gate) * up).astype(BF16), wo_ref[c * th:(c + 1) * th, :])
    return acc


def _post_body(ka, th, x_ref, a_ref, b_ref, k_ref, v_ref, wmix_ref, gx_ref, wq_ref, qn_ref,
               wo_ref, gf_ref, wi_ref, wf_ref, o_ref):
    a, b = a_ref[...], b_ref[...]
    mix = [_dot(a, wmix_ref[:ka, c:c + TH_FFN]) + _dot(b, wmix_ref[ka:, c:c + TH_FFN])
           for c in range(0, wmix_ref.shape[1], TH_FFN)]
    x = x_ref[...] + jnp.concatenate(mix, axis=1)
    x = _cross_update(x, gx_ref, wq_ref, qn_ref, k_ref, v_ref, wo_ref)
    o_ref[...] = _ffn_update(x, gf_ref, wi_ref, wf_ref, wf_ref.shape[0], th)


def _post_mixer(x, a, b, k, v, wmix, gx, wq, qn, wo, gf, wi, wf, batch, tm, th):
    m, d = x.shape
    ka, kb = a.shape[1], b.shape[1]
    per = (m // batch) // tm
    nm = k.arr.shape[1] // batch
    hw = X_HEADS * X_DH
    row = lambda i: (i, 0)
    kv = pl.BlockSpec((None, nm, hw), lambda i: (k.idx, i // per, 0))
    return _call(
        functools.partial(_post_body, ka, th), (m // tm,),
        [(x, pl.BlockSpec((tm, d), row)), (a, pl.BlockSpec((tm, ka), row)),
         (b, pl.BlockSpec((tm, kb), row)), (k.arr, kv), (v.arr, kv)],
        [_whole(wmix, True), _whole(gx), _whole(wq, True), _whole(qn), _whole(wo, True), _whole(gf),
         _whole(wi, True), _whole(wf, True)],
        pl.BlockSpec((tm, d), row), jax.ShapeDtypeStruct((m, d), F32), ("parallel",), "post_mixer")


def _row(v):
    return v.reshape(1, -1).astype(F32)


def _block_diag(w):
    n, d, e = w.shape
    eye = jnp.eye(n, dtype=w.dtype)
    return (eye[:, None, :, None] * w[:, :, None, :]).reshape(n * d, n * e)


EV_QKV = 3 * A_HEADS * A_D
EV_Z = A_HEADS * A_D


def _even_w_in(w_in):
    o = EV_QKV + EV_Z
    w_bd = w_in[..., o:o + 2 * A_HEADS]
    w_bd = jnp.pad(w_bd, ((0, 0), (0, 0), (0, LANES - 2 * A_HEADS)))
    return jnp.concatenate([w_in[..., :o], w_in[..., o + 2 * A_HEADS:], w_bd], axis=-1).astype(BF16)


def _even_layer(x, batch, tm, tb, norm_g, w, conv_qkv, a_log, dt_bias, o_norm, conv_b_w, conv_b_b,
                gate_a_w, gate_a_b, gate_x_w, gate_x_b, lru_l):
    qkv, z, xb, gb, bd = _norm_proj(x, norm_g, w, [EV_QKV, EV_Z, B_WIDTH, B_WIDTH, LANES],
                                    [F32] * 5, tm)
    pad = jnp.zeros((A_HEADS,), F32)
    avec = _row(jnp.pad(jnp.concatenate([pad, a_log.astype(F32)]), (0, LANES - 2 * A_HEADS)))
    dvec = _row(jnp.pad(jnp.concatenate([pad, dt_bias.astype(F32)]), (0, LANES - 2 * A_HEADS)))
    y_a = _delta_rule(qkv, z, bd, conv_qkv.astype(F32), avec, dvec, _row(o_norm), batch, tb)
    y_b = _rglru(xb, gb, conv_b_w.astype(F32), _row(conv_b_b), _block_diag(gate_a_w).astype(BF16),
                 _row(gate_a_b), _block_diag(gate_x_w).astype(BF16), _row(gate_x_b), _row(lru_l),
                 batch, min(TB_LRU, tm))
    return y_a, y_b


def _head_pad_cols(w, heads, real, total):
    lead = w.shape[:-1]
    w = w.reshape(lead + (heads, real))
    pad = [(0, 0)] * (w.ndim - 1) + [(0, total - real)]
    return jnp.pad(w, pad).reshape(lead + (heads * total,))


def _odd_weights(w_in, w_uq, w_ukv):
    hw = C_HEADS * 2 * C_DK
    cut = 3 * hw + Q_LORA + KV_LORA
    k_rope_w = jnp.pad(w_in[..., cut:], ((0, 0), (0, 0), (D_NOPE, LANES - D_QK)))
    w = jnp.concatenate([w_in[..., :cut], k_rope_w], axis=-1).astype(BF16)
    wuq = _head_pad_cols(w_uq, D_HEADS, D_QK, LANES).astype(BF16)
    kvr = w_ukv.reshape(w_ukv.shape[:-1] + (D_HEADS, D_NOPE + LANES))
    flat = lambda a: a.reshape(a.shape[:-2] + (-1,))
    wukv = jnp.concatenate([_head_pad_cols(flat(kvr[..., :D_NOPE]), D_HEADS, D_NOPE, LANES),
                            flat(kvr[..., D_NOPE:])], axis=-1).astype(BF16)
    return w, wuq, wukv


def _odd_layer(x, rope, pos_col, pos_row, batch, tm, tq, layer_idx, norm_g, w, c_q_norm, c_k_norm,
               lam_q1, lam_k1, lam_q2, lam_k2, c_sub_norm, q_lat_norm, wuq, kv_lat_norm, wukv,
               d_q_norm, d_k_norm):
    tile = lambda v, n: _row(jnp.tile(v.astype(F32), n))
    padn = lambda v: jnp.pad(v.astype(F32), (0, LANES - D_QK))
    qc, kc, vc, qd, kd, vd = _odd_in(
        x, rope[0], rope[1], norm_g, w, tile(c_q_norm, 2 * C_HEADS), tile(c_k_norm, 2 * C_HEADS),
        _row(q_lat_norm), wuq, _row(kv_lat_norm), wukv, tile(padn(d_q_norm), D_HEADS),
        tile(padn(d_k_norm), D_HEADS), tm, (min(TK_DIFF, tm), min(TK_MLA, tm)))
    lam_init = 0.8 - 0.6 * math.exp(-0.3 * layer_idx)
    lamv = jnp.stack([lam_q1, lam_k1, lam_q2, lam_k2]).astype(F32)
    nb_d = NB_MLA if batch % NB_MLA == 0 else 1
    yc = _flash(qc, kc, vc, batch, tq, nb_d, 2, (lamv, _row(c_sub_norm), pos_col, pos_row), lam_init)
    yd = _flash(qd, kd, vd, batch, tq, nb_d, 1)
    return yc, yd


def kernel(x, mem, positions, norm_mix, norm_x, norm_mem, x_wq, x_wkv, x_q_norm, x_k_norm, x_wo, norm_ffn, ffn_w_in, ffn_w_out, ev_w_in, ev_conv_qkv, ev_a_log, ev_dt_bias, ev_o_norm, ev_conv_b_w, ev_conv_b_b, ev_gate_a_w, ev_gate_a_b, ev_gate_x_w, ev_gate_x_b, ev_lru_l, ev_w_out, od_w_in, od_c_q_norm, od_c_k_norm, od_lam_q1, od_lam_k1, od_lam_q2, od_lam_k2, od_c_sub_norm, od_q_lat_norm, od_w_uq, od_kv_lat_norm, od_w_ukv, od_d_q_norm, od_d_k_norm, od_w_out):
    batch, seq, d = x.shape
    depth = norm_mix.shape[0]
    m = batch * seq
    tm = min(512, seq)
    tb = min(256, seq)
    tq = min(T_Q, seq)
    hid = ffn_w_out.shape[1]
    th = TH_FFN if hid % TH_FFN == 0 else hid
    tm_post = min(TM_POST, seq)
    xf = x.reshape(m, d).astype(F32)
    mem2 = mem.reshape(-1, d).astype(F32)
    pos_col = positions.reshape(m, 1).astype(jnp.int32)
    pos_row = positions.reshape(batch, seq // tq, 1, tq).astype(jnp.int32)
    rope = _rope_tables(pos_col, tm) if depth > 1 else None
    vec = lambda a: a.astype(F32).reshape(a.shape[0], 1, -1)
    n_x, n_mem, n_ffn, n_mix = vec(norm_x), vec(norm_mem), vec(norm_ffn), vec(norm_mix)
    ev_w = _even_w_in(ev_w_in)
    od_w, od_wuq, od_wukv = _odd_weights(od_w_in, od_w_uq, od_w_ukv)
    qn_all = vec(jnp.tile(x_q_norm, (1, X_HEADS)))
    kn_all = vec(jnp.tile(x_k_norm, (1, X_HEADS)))
    wq_all, wkv_all, wo_all = x_wq.astype(BF16), x_wkv.astype(BF16), x_wo.astype(BF16)
    wi_all, wf_all = ffn_w_in.astype(BF16), ffn_w_out.astype(BF16)
    wmix_all = (ev_w_out.astype(BF16), od_w_out.astype(BF16))
    k_mem, v_mem = _mem_kv(mem2, n_mem[:depth], wkv_all[:depth], kn_all[:depth], batch)
    for l in range(depth):
        i = l // 2
        if l % 2 == 0:
            y_a, y_b = _even_layer(xf, batch, tm, tb, _Layer(n_mix, l), _Layer(ev_w, i), ev_conv_qkv[i], ev_a_log[i],
                                   ev_dt_bias[i], ev_o_norm[i], ev_conv_b_w[i], ev_conv_b_b[i],
                                   ev_gate_a_w[i], ev_gate_a_b[i], ev_gate_x_w[i], ev_gate_x_b[i],
                                   ev_lru_l[i])
        else:
            y_a, y_b = _odd_layer(xf, rope, pos_col, pos_row, batch, tm, tq, l, _Layer(n_mix, l), _Layer(od_w, i),
                                  od_c_q_norm[i], od_c_k_norm[i], od_lam_q1[i], od_lam_k1[i], od_lam_q2[i],
                                  od_lam_k2[i], od_c_sub_norm[i], od_q_lat_norm[i], _Layer(od_wuq, i),
                                  od_kv_lat_norm[i], _Layer(od_wukv, i), od_d_q_norm[i], od_d_k_norm[i])
        xf = _post_mixer(xf, y_a, y_b, _Layer(k_mem, l), _Layer(v_mem, l), _Layer(wmix_all[l % 2], i), _Layer(n_x, l),
                         _Layer(wq_all, l), _Layer(qn_all, l), _Layer(wo_all, l), _Layer(n_ffn, l),
                         _Layer(wi_all, l), _Layer(wf_all, l), batch, tm_post, th)
    return xf.reshape(batch, seq, d).astype(x.dtype)
```

```python
import functools
import math
from typing import NamedTuple

import jax
import jax.numpy as jnp
import numpy as np
from jax import lax
from jax.experimental import pallas as pl
from jax.experimental.pallas import tpu as pltpu

F32 = jnp.float32
BF16 = jnp.bfloat16

NORM_EPS = 1e-6
NEG_INF = -1e30
CHUNK = 64
LANES = 128
SUBLANES = 8
LRU_C = 8.0
ROPE_THETA = 10000.0
VMEM_LIMIT = 56 * 1024 * 1024

A_HEADS = 4
A_D = 128
B_WIDTH = 512
B_BLOCKS = 8
C_HEADS = 4
C_DK = 64
D_HEADS = 4
D_NOPE = 64
D_ROPE = 32
D_QK = D_NOPE + D_ROPE
Q_LORA = 256
KV_LORA = 128
X_HEADS = 4
X_DH = 128


def _params(*sem):
    return pltpu.CompilerParams(dimension_semantics=sem, vmem_limit_bytes=VMEM_LIMIT)


def _rms(x, g):
    ms = jnp.mean(x * x, axis=-1, keepdims=True)
    return x * lax.rsqrt(ms + NORM_EPS) * g


def _dot(a, b):
    return jnp.dot(a, b, preferred_element_type=F32)


def _dot_nt(a, b):
    return lax.dot_general(a, b, (((1,), (1,)), ((), ())), preferred_element_type=F32)


def _sigmoid(x):
    return 0.5 * jnp.tanh(0.5 * x) + 0.5


def _silu(x):
    h = 0.5 * x
    return h + h * jnp.tanh(h)


def _softplus(x):
    return jnp.maximum(x, 0.0) + jnp.log(1.0 + jnp.exp(-jnp.abs(x)))


def _full(shape):
    return pl.BlockSpec(shape, lambda *_: (0,) * len(shape))


class _Layer(NamedTuple):
    arr: jax.Array
    idx: int


def _whole(a, resident=False):
    kw = dict(pipeline_mode=pl.Buffered(1)) if resident else {}
    if isinstance(a, _Layer):
        tail = a.arr.shape[1:]
        return a.arr, pl.BlockSpec((None,) + tail, lambda *_: (a.idx,) + (0,) * len(tail), **kw)
    return a, pl.BlockSpec(a.shape, lambda *_: (0,) * a.ndim, **kw)


def _call(body, grid, tiled, whole, out_specs, out_shape, sem, name, scratch=()):
    ops = list(tiled) + list(whole)
    return pl.pallas_call(
        body, grid=grid, in_specs=[s for _, s in ops], out_specs=out_specs, out_shape=out_shape,
        scratch_shapes=list(scratch), compiler_params=_params(*sem), name=name,
    )(*[a for a, _ in ops])


def _norm_proj_body(splits, x_ref, g_ref, w_ref, *out_refs):
    h = _rms(x_ref[...], g_ref[...]).astype(BF16)
    for (start, width), o_ref in zip(splits, out_refs):
        o_ref[...] = _dot(h, w_ref[:, start:start + width]).astype(o_ref.dtype)


def _norm_proj(x, g, w, widths, dtypes, tm):
    m, d = x.shape
    starts = np.concatenate([[0], np.cumsum(widths)[:-1]]).tolist()
    splits = tuple(zip(starts, widths))
    return _call(
        functools.partial(_norm_proj_body, splits), (m // tm,),
        [(x, pl.BlockSpec((tm, d), lambda i: (i, 0)))], [_whole(g), _whole(w)],
        [pl.BlockSpec((tm, wd), lambda i: (i, 0)) for wd in widths],
        [jax.ShapeDtypeStruct((m, wd), dt) for wd, dt in zip(widths, dtypes)],
        ("parallel",), "norm_proj")


HALO = 8
TB_LRU = 1024


def _causal_conv(x_ref, w_ref, hist_ref, cols=slice(None)):
    x = x_ref[:, cols]
    tb = x.shape[0]
    k = w_ref.shape[0]
    full = jnp.concatenate([hist_ref[:, cols], x], axis=0)
    hist_ref[:, cols] = x[tb - HALO:, :]
    y = x * w_ref[k - 1:k, cols]
    for j in range(k - 1):
        y = y + pltpu.roll(full, k - 1 - j, 0)[HALO:, :] * w_ref[j:j + 1, cols]
    return y


def _l2_heads(x, nblk):
    outs = []
    for h in range(nblk):
        xh = x[:, h * LANES:(h + 1) * LANES]
        outs.append(xh * lax.rsqrt(jnp.sum(xh * xh, axis=-1, keepdims=True) + NORM_EPS))
    return jnp.concatenate(outs, axis=1)


def _delta_body(tb, qkv_ref, z_ref, bd_ref, cw_ref, avec_ref, dvec_ref, onorm_ref, tri_ref,
                y_ref, hist_ref, state_ref):
    @pl.when(pl.program_id(1) == 0)
    def _():
        state_ref[...] = jnp.zeros_like(state_ref)
        hist_ref[...] = jnp.zeros_like(hist_ref)

    nh, d = A_HEADS, A_D
    qk_w = nh * d

    def act(i):
        return _silu(_causal_conv(qkv_ref, cw_ref, hist_ref, slice(i * qk_w, (i + 1) * qk_w)))

    q_all = _l2_heads(act(0), nh) * d ** -0.5
    k_all = _l2_heads(act(1), nh)
    v_all = act(2)
    bd = bd_ref[...]
    beta_all = _sigmoid(bd)
    g_all = -jnp.exp(avec_ref[...]) * _softplus(bd + dvec_ref[...])
    cum_all = jnp.dot(tri_ref[...], g_all, precision=lax.Precision.HIGHEST,
                      preferred_element_type=F32)
    cum_t = cum_all.T

    ri = lax.broadcasted_iota(jnp.int32, (CHUNK, CHUNK), 0)
    ci = lax.broadcasted_iota(jnp.int32, (CHUNK, CHUNK), 1)
    incl = ri >= ci
    strict = ri > ci
    eye = jnp.where(ri == ci, 1.0, 0.0).astype(F32)

    nc = tb // CHUNK
    probs = [(h, c) for c in range(nc) for h in range(nh)]
    a_mat, rhs_wu, q_dec, k_dec_t, qk, neg_m, last = {}, {}, {}, {}, {}, {}, {}
    for pr in probs:
        h, c = pr
        rows = slice(c * CHUNK, (c + 1) * CHUNK)
        cols = slice(h * d, (h + 1) * d)
        q, k, v = q_all[rows, cols], k_all[rows, cols], v_all[rows, cols]
        beta = beta_all[rows, h:h + 1]
        cum = cum_all[rows, nh + h:nh + h + 1]
        cum_row = cum_t[nh + h:nh + h + 1, c * CHUNK:(c + 1) * CHUNK]
        cum_last = cum[CHUNK - 1:CHUNK, :]
        decay = jnp.where(incl, jnp.exp(jnp.where(incl, cum - cum_row, 0.0)), 0.0)
        e_cum = jnp.exp(cum)
        kb = k * beta
        a_mat[pr] = _dot_nt(jnp.concatenate([q, kb], axis=0).astype(BF16), k.astype(BF16))
        qk[pr] = jnp.where(incl, a_mat[pr][:CHUNK] * decay, 0.0).astype(BF16)
        neg_m[pr] = jnp.where(strict, -a_mat[pr][CHUNK:] * decay, 0.0)
        rhs_wu[pr] = jnp.concatenate([kb * e_cum, v * beta], axis=1).astype(BF16)
        q_dec[pr] = q * e_cum
        k_dec_t[pr] = (k * jnp.exp(cum_last - cum)).T.astype(BF16)
        last[pr] = jnp.exp(cum_last)
    pw = dict(neg_m)
    t_inv = {pr: eye + neg_m[pr] for pr in probs}
    for _ in range(CHUNK.bit_length() - 2):
        for pr in probs:
            p_bf = pw[pr].astype(BF16)
            pw[pr] = _dot(p_bf, p_bf)
        for pr in probs:
            t_inv[pr] = _dot(t_inv[pr].astype(BF16), (eye + pw[pr]).astype(BF16))
    wu = {pr: _dot(t_inv[pr].astype(BF16), rhs_wu[pr]) for pr in probs}

    state = [state_ref[h] for h in range(nh)]
    outs = [[] for _ in range(nh)]
    for c in range(nc):
        ws = [_dot(jnp.concatenate([wu[h, c][:, :d], q_dec[h, c]], axis=0).astype(BF16),
                   state[h].astype(BF16)) for h in range(nh)]
        v_new = [(wu[h, c][:, d:] - ws[h][:CHUNK]).astype(BF16) for h in range(nh)]
        for h in range(nh):
            outs[h].append(ws[h][CHUNK:] + _dot(qk[h, c], v_new[h]))
        state = [state[h] * last[h, c] + _dot(k_dec_t[h, c], v_new[h]) for h in range(nh)]
    for h in range(nh):
        state_ref[h] = state[h]
        o = jnp.concatenate(outs[h], axis=0)
        o = _rms(o, onorm_ref[...]) * _silu(z_ref[:, h * d:(h + 1) * d])
        y_ref[:, h * d:(h + 1) * d] = o.astype(y_ref.dtype)


def _delta_rule(qkv, z, bd, conv_w, avec, dvec, onorm, batch, tb):
    m, c3 = qkv.shape
    s = m // batch
    nsb = s // tb
    cid = np.arange(tb) // CHUNK
    tri = jnp.asarray(((cid[:, None] == cid[None, :]) &
                       (np.arange(tb)[:, None] >= np.arange(tb)[None, :])).astype(np.float32))
    vw = A_HEADS * A_D
    row = lambda b, j: (b * nsb + j, 0)
    return pl.pallas_call(
        functools.partial(_delta_body, tb),
        grid=(batch, nsb),
        in_specs=[pl.BlockSpec((tb, c3), row), pl.BlockSpec((tb, vw), row),
                  pl.BlockSpec((tb, LANES), row), _full(conv_w.shape), _full(avec.shape),
                  _full(dvec.shape), _full(onorm.shape), _full(tri.shape)],
        out_specs=pl.BlockSpec((tb, vw), row),
        out_shape=jax.ShapeDtypeStruct((m, vw), BF16),
        scratch_shapes=[pltpu.VMEM((HALO, c3), F32),
                        pltpu.VMEM((A_HEADS, A_D, A_D), F32)],
        compiler_params=_params("parallel", "arbitrary"),
        name="delta_rule",
    )(qkv, z, bd, conv_w, avec, dvec, onorm, tri)


def _gelu_tanh(x):
    return 0.5 * x * (1.0 + jnp.tanh(math.sqrt(2.0 / math.pi) * (x + 0.044715 * (x * x * x))))


def _lru_body(tb, xb_ref, gb_ref, cw_ref, cb_ref, wa_ref, ba_ref, wx_ref, bx_ref, l_ref,
              y_ref, hist_ref, h_ref):
    @pl.when(pl.program_id(1) == 0)
    def _():
        h_ref[...] = jnp.zeros_like(h_ref)
        hist_ref[...] = jnp.zeros_like(hist_ref)

    xc = _causal_conv(xb_ref, cw_ref, hist_ref) + cb_ref[...]
    xc_bf = xc.astype(BF16)
    r = _sigmoid(_dot(xc_bf, wa_ref[...]) + ba_ref[...])
    i = _sigmoid(_dot(xc_bf, wx_ref[...]) + bx_ref[...])
    log_a = (-LRU_C) * r * _softplus(-l_ref[...])
    a = jnp.exp(log_a)
    th = jnp.tanh(log_a)
    u = jnp.sqrt(-2.0 * th / (1.0 - th)) * (i * xc)
    wdt = a.shape[1]
    a3 = a.reshape(tb // SUBLANES, SUBLANES, wdt)
    u3 = u.reshape(tb // SUBLANES, SUBLANES, wdt)
    sub = lax.broadcasted_iota(jnp.int32, a3.shape, 1)
    shift = 1
    while shift < SUBLANES:
        keep = sub >= shift
        a_prev = jnp.where(keep, pltpu.roll(a3, shift, 1), 1.0)
        u_prev = jnp.where(keep, pltpu.roll(u3, shift, 1), 0.0)
        u3 = u3 + a3 * u_prev
        a3 = a3 * a_prev
        shift *= 2
    carry = h_ref[...]
    groups = []
    for r in range(tb // SUBLANES):
        h_r = u3[r] + a3[r] * carry
        carry = h_r[SUBLANES - 1:SUBLANES, :]
        groups.append(h_r)
    h_ref[...] = carry
    hs = jnp.concatenate(groups, axis=0)
    y_ref[...] = (_gelu_tanh(gb_ref[...]) * hs).astype(y_ref.dtype)


def _rglru(xb, gb, cw, cb, wa, ba, wx, bx, lru_l, batch, tb):
    m, wdt = xb.shape
    nsb = (m // batch) // tb
    row = lambda b, j: (b * nsb + j, 0)
    return pl.pallas_call(
        functools.partial(_lru_body, tb),
        grid=(batch, nsb),
        in_specs=[pl.BlockSpec((tb, wdt), row), pl.BlockSpec((tb, wdt), row),
                  _full(cw.shape), _full(cb.shape), _full(wa.shape), _full(ba.shape),
                  _full(wx.shape), _full(bx.shape), _full(lru_l.shape)],
        out_specs=pl.BlockSpec((tb, wdt), row),
        out_shape=jax.ShapeDtypeStruct((m, wdt), BF16),
        scratch_shapes=[pltpu.VMEM((HALO, wdt), F32), pltpu.VMEM((1, wdt), F32)],
        compiler_params=_params("parallel", "arbitrary"),
        name="rglru",
    )(xb, gb, cw, cb, wa, ba, wx, bx, lru_l)


def _head_norm(x, nblk, denom):
    outs = []
    for h in range(nblk):
        xh = x[:, h * LANES:(h + 1) * LANES]
        ss = jnp.sum(xh * xh, axis=-1, keepdims=True)
        outs.append(xh * lax.rsqrt(ss * (1.0 / denom) + NORM_EPS))
    return jnp.concatenate(outs, axis=1)


def _group_norm(x, ones_ref, denom):
    ss = _dot((x * x).astype(BF16), ones_ref[...])
    return x * lax.rsqrt(ss * (1.0 / denom) + NORM_EPS)


def _group_ones(width, group):
    gid = np.arange(width) // group
    return jnp.asarray(gid[:, None] == gid[None, :], dtype=BF16)


def _rope_table_body(pos_ref, freq_ref, cos_ref, sin_ref):
    lane = lax.broadcasted_iota(jnp.int32, (1, LANES), 1)
    ang = pos_ref[...].astype(F32) * freq_ref[...]
    is_rope = (lane >= D_NOPE) & (lane < D_QK)
    cos_ref[...] = jnp.where(is_rope, jnp.cos(ang), 1.0)
    sin_raw = jnp.sin(ang)
    sin_ref[...] = jnp.where(is_rope, jnp.where(lane < D_NOPE + D_ROPE // 2, -sin_raw, sin_raw), 0.0)


def _rope_tables(pos, tm):
    m = pos.shape[0]
    half = D_ROPE // 2
    inv_freq = ROPE_THETA ** (-jnp.arange(half, dtype=F32) / half)
    freq = jnp.concatenate([jnp.zeros((D_NOPE,), F32), inv_freq, inv_freq,
                            jnp.zeros((LANES - D_QK,), F32)]).reshape(1, LANES)
    row = lambda i: (i, 0)
    return _call(
        _rope_table_body, (m // tm,), [(pos, pl.BlockSpec((tm, 1), row))], [_whole(freq)],
        [pl.BlockSpec((tm, LANES), row)] * 2, [jax.ShapeDtypeStruct((m, LANES), F32)] * 2,
        ("parallel",), "rope_tables")


def _rope_blocks(x, nblk, cos, sin):
    lane = lax.broadcasted_iota(jnp.int32, (1, LANES), 1)
    first_half = lane < D_NOPE + D_ROPE // 2
    outs = []
    for h in range(nblk):
        xh = x[:, h * LANES:(h + 1) * LANES]
        partner = jnp.where(first_half, pltpu.roll(xh, LANES - D_ROPE // 2, 1),
                            pltpu.roll(xh, D_ROPE // 2, 1))
        outs.append(xh * cos + partner * sin)
    return jnp.concatenate(outs, axis=1)


def _store_vt(vt_ref, v, t):
    vt = v.T
    for j in range(v.shape[0] // t):
        vt_ref[j] = vt[:, j * t:(j + 1) * t].astype(vt_ref.dtype)


def _odd_in_body(tks, x_ref, cos_ref, sin_ref, g_ref, w_ref, cqn_ref, ckn_ref, qln_ref, wuq_ref, kvn_ref,
                 wukv_ref, dqn_ref, dkn_ref, ones_c_ref, ones_d_ref,
                 qc_ref, kc_ref, vc_ref, qd_ref, kd_ref, vd_ref):
    hw = C_HEADS * LANES
    h = _rms(x_ref[...], g_ref[...]).astype(BF16)
    qc = _dot(h, w_ref[:, 0:hw])
    qc_ref[...] = (_group_norm(qc, ones_c_ref, C_DK) * (cqn_ref[...] * (C_DK ** -0.5 * LOG2E))).astype(BF16)
    kc = _dot(h, w_ref[:, hw:2 * hw])
    kc_ref[...] = (_group_norm(kc, ones_c_ref, C_DK) * ckn_ref[...]).astype(BF16)
    _store_vt(vc_ref, _dot(h, w_ref[:, 2 * hw:3 * hw]), tks[0])

    cos = cos_ref[...]
    sin = sin_ref[...]
    off = 3 * hw
    q_lat = _dot(h, w_ref[:, off:off + Q_LORA])
    qd = _dot(_rms(q_lat, qln_ref[...]).astype(BF16), wuq_ref[...])
    qd = _group_norm(qd, ones_d_ref, D_QK) * (dqn_ref[...] * (D_QK ** -0.5 * LOG2E))
    qd_ref[...] = _rope_blocks(qd, D_HEADS, cos, sin).astype(BF16)

    off += Q_LORA
    kv_lat = _dot(h, w_ref[:, off:off + KV_LORA])
    k_rope = _dot(h, w_ref[:, off + KV_LORA:off + KV_LORA + LANES])
    kvd = _dot(_rms(kv_lat, kvn_ref[...]).astype(BF16), wukv_ref[...])
    dhw = D_HEADS * LANES
    kd = kvd[:, :dhw] + jnp.concatenate([k_rope] * D_HEADS, axis=1)
    kd = _group_norm(kd, ones_d_ref, D_QK) * dkn_ref[...]
    kd_ref[...] = _rope_blocks(kd, D_HEADS, cos, sin).astype(BF16)
    _store_vt(vd_ref, kvd[:, dhw:], tks[1])


def _odd_in(x, cos, sin, g, w, cqn, ckn, qln, wuq, kvn, wukv, dqn, dkn, tm, tks):
    m, d = x.shape
    hw = C_HEADS * LANES
    row = lambda i: (i, 0)
    small = [g, w, cqn, ckn, qln, wuq, kvn, wukv, dqn, dkn, _group_ones(hw, C_DK), _group_ones(hw, LANES)]
    rspec = pl.BlockSpec((tm, hw), row)
    tspec = [pl.BlockSpec((tm // t, hw, t), lambda i: (i, 0, 0)) for t in tks]
    rshape = jax.ShapeDtypeStruct((m, hw), BF16)
    tshape = [jax.ShapeDtypeStruct((m // t, hw, t), BF16) for t in tks]
    tab = pl.BlockSpec((tm, LANES), row)
    return _call(
        functools.partial(_odd_in_body, tks), (m // tm,),
        [(x, pl.BlockSpec((tm, d), row)), (cos, tab), (sin, tab)],
        [_whole(a) for a in small],
        [rspec, rspec, tspec[0], rspec, rspec, tspec[1]],
        [rshape, rshape, tshape[0], rshape, rshape, tshape[1]],
        ("parallel",), "odd_in")


LOG2E = math.log2(math.e)
T_Q = 256
TK_DIFF = 256
TK_MLA = 512
SUM_ROWS = 16
QK_AHEAD = 6
NB_MLA = 2
TM_ODD = 1024
TM_ROPE = 2048
TM_POST = 1024
TH_FFN = 256


def _flash_body(n_maps, alibi, t, nb, lam_init, *refs):
    if alibi:
        lam_ref, sub_ref, q_ref, k_ref, vt_ref, pq_ref, pk_ref, o_ref = refs
    else:
        q_ref, k_ref, vt_ref, o_ref = refs
    seq = q_ref.shape[0] // nb
    nq = seq // t
    nh = q_ref.shape[1] // LANES
    tk = vt_ref.shape[2]
    ng = seq // tk
    lane = lax.broadcasted_iota(jnp.int32, (1, LANES), 1)
    key_i = lax.broadcasted_iota(jnp.int32, (tk, t), 0)
    qry_i = lax.broadcasted_iota(jnp.int32, (tk, t), 1)
    shift = CHUNK.bit_length() - 1
    chains = [(b, h, mp) for b in range(nb) for h in range(nh) for mp in range(n_maps)]
    ones_rows = jnp.ones((SUM_ROWS, tk), BF16)
    hcols = lambda h: slice(h * LANES, (h + 1) * LANES)

    def q_block(qi, _):
        qm = {}
        for b in range(nb):
            rows = pl.ds(pl.multiple_of(b * seq + qi * t, t), t)
            for h in range(nh):
                q = q_ref[rows, hcols(h)]
                for mp in range(n_maps):
                    qm[b, h, mp] = (jnp.where((lane < C_DK) == (mp == 0), q, jnp.zeros_like(q))
                                    if n_maps == 2 else q)

        def group(g, carry, masked):
            krows = [pl.ds(pl.multiple_of(b * seq + g * tk, tk), tk) for b in range(nb)]
            if masked:
                allowed = ((key_i + g * tk) >> shift) <= ((qry_i + qi * t) >> shift)
            bias = {}

            def scores(ch):
                b, h, _ = ch
                s = _dot_nt(k_ref[krows[b], hcols(h)], qm[ch])
                if alibi:
                    if (b, h) not in bias:
                        dist = jnp.abs(pk_ref[krows[b], :].astype(F32) - pq_ref[b, qi].astype(F32))
                        bias[b, h] = (2.0 ** (-2 * (h + 1)) * LOG2E) * dist
                    s = s - bias[b, h]
                return jnp.where(allowed, s, NEG_INF) if masked else s

            s = {ch: scores(ch) for ch in chains[:QK_AHEAD]}
            new = []
            for i, (ch, (m_i, acc)) in enumerate(zip(chains, carry)):
                if i + QK_AHEAD < len(chains):
                    nxt = chains[i + QK_AHEAD]
                    s[nxt] = scores(nxt)
                b, h, _ = ch
                vt_ext = jnp.concatenate([vt_ref[b * ng + g, hcols(h), :], ones_rows], axis=0)
                s_ch = s.pop(ch)
                m_new = jnp.maximum(m_i, jnp.max(s_ch, axis=0, keepdims=True))
                alpha = jnp.exp2(m_i - m_new)
                p = jnp.exp2((s_ch - m_new).astype(BF16))
                new.append((m_new, alpha * acc + _dot(vt_ext, p)))
            return tuple(new)

        init = tuple((jnp.full((1, t), NEG_INF, F32), jnp.zeros((LANES + SUM_ROWS, t), F32))
                     for _ in chains)
        g_last = (qi * t) // tk
        carry = lax.fori_loop(0, g_last, lambda g, c: group(g, c, False), init)
        carry = group(g_last, carry, True)
        outs = {ch: acc[:LANES] / acc[LANES:LANES + 1] for ch, (_, acc) in zip(chains, carry)}
        if n_maps == 2:
            lv = lam_ref[...]
            lam = (jnp.exp(jnp.sum(lv[0:1] * lv[1:2], axis=-1, keepdims=True))
                   - jnp.exp(jnp.sum(lv[2:3] * lv[3:4], axis=-1, keepdims=True)) + lam_init)
        for b in range(nb):
            rows = pl.ds(pl.multiple_of(b * seq + qi * t, t), t)
            for h in range(nh):
                if n_maps == 2:
                    o = (outs[b, h, 0] - lam * outs[b, h, 1]).T
                    o = _rms(o, sub_ref[...]) * (1.0 - lam_init)
                else:
                    o = outs[b, h, 0].T
                o_ref[rows, hcols(h)] = o.astype(o_ref.dtype)
        return 0

    lax.fori_loop(0, nq, q_block, 0)


def _flash(q, k, vt, batch, t, nb, n_maps, extras=None, lam_init=0.0):
    m, hw = q.shape
    s = m // batch
    nq = s // t
    tk = vt.shape[2]
    alibi = extras is not None
    spec = pl.BlockSpec((nb * s, hw), lambda b: (b, 0))
    in_specs = [spec, spec, pl.BlockSpec((nb * s // tk, hw, tk), lambda b: (b, 0, 0))]
    args = [q, k, vt]
    if alibi:
        lamv, subn, pos_col, pos_row = extras
        in_specs = [_full(lamv.shape), _full(subn.shape)] + in_specs + [
            pl.BlockSpec((nb, nq, 1, t), lambda b: (b, 0, 0, 0)),
            pl.BlockSpec((nb * s, 1), lambda b: (b, 0))]
        args = [lamv, subn] + args + [pos_row, pos_col]
    return pl.pallas_call(
        functools.partial(_flash_body, n_maps, alibi, t, nb, lam_init),
        grid=(batch // nb,),
        in_specs=in_specs,
        out_specs=spec,
        out_shape=jax.ShapeDtypeStruct((m, hw), BF16),
        compiler_params=_params("parallel"),
        name="flash_diff" if alibi else "flash_mla",
    )(*args)


def _mem_kv_body(mem_ref, g_ref, w_ref, kn_ref, k_ref, v_ref):
    hw = X_HEADS * X_DH
    mn = _rms(mem_ref[...], g_ref[...]).astype(BF16)
    k = _dot(mn, w_ref[:, :hw])
    k_ref[...] = (_head_norm(k, X_HEADS, X_DH) * kn_ref[...]).astype(BF16)
    v_ref[...] = _dot(mn, w_ref[:, hw:]).astype(BF16)


def _mem_kv(mem2, g, w, kn):
    rows, d = mem2.shape
    layers = w.shape[0]
    hw = X_HEADS * X_DH
    per_layer = lambda a: (a, pl.BlockSpec((None,) + a.shape[1:], lambda l: (l, 0, 0)))
    return _call(
        _mem_kv_body, (layers,),
        [(mem2, pl.BlockSpec((rows, d), lambda l: (0, 0))), per_layer(g), per_layer(w), per_layer(kn)],
        [],
        [pl.BlockSpec((None, rows, hw), lambda l: (l, 0, 0))] * 2,
        [jax.ShapeDtypeStruct((layers, rows, hw), BF16)] * 2,
        ("parallel",), "mem_kv")


def _cross_update(x, g_ref, wq_ref, qn_ref, k_ref, v_ref, wo_ref):
    q = _dot(_rms(x, g_ref[...]).astype(BF16), wq_ref[...])
    q = (_head_norm(q, X_HEADS, X_DH) * (qn_ref[...] * X_DH ** -0.5)).astype(BF16)
    outs = []
    for h in range(X_HEADS):
        blk = slice(h * X_DH, (h + 1) * X_DH)
        s = _dot_nt(q[:, blk], k_ref[:, blk])
        p = jnp.exp(s - jnp.max(s, axis=-1, keepdims=True))
        l = jnp.sum(p, axis=-1, keepdims=True)
        outs.append((_dot(p.astype(BF16), v_ref[:, blk]) / l).astype(BF16))
    return x + _dot(jnp.concatenate(outs, axis=1), wo_ref[...])


def _ffn_update(x, g_ref, wi_ref, wo_ref, hid, th):
    h = _rms(x, g_ref[...]).astype(BF16)
    acc = x
    for c in range(hid // th):
        gate = _dot(h, wi_ref[:, c * th:(c + 1) * th])
        up = _dot(h, wi_ref[:, hid + c * th:hid + (c + 1) * th])
        acc = acc + _dot((_silu(gate) * up).astype(BF16), wo_ref[c * th:(c + 1) * th, :])
    return acc


def _post_body(ka, th, x_ref, a_ref, b_ref, k_ref, v_ref, wmix_ref, gx_ref, wq_ref, qn_ref,
               wo_ref, gf_ref, wi_ref, wf_ref, o_ref):
    a, b = a_ref[...], b_ref[...]
    mix = [_dot(a, wmix_ref[:ka, c:c + TH_FFN]) + _dot(b, wmix_ref[ka:, c:c + TH_FFN])
           for c in range(0, wmix_ref.shape[1], TH_FFN)]
    x = x_ref[...] + jnp.concatenate(mix, axis=1)
    x = _cross_update(x, gx_ref, wq_ref, qn_ref, k_ref, v_ref, wo_ref)
    o_ref[...] = _ffn_update(x, gf_ref, wi_ref, wf_ref, wf_ref.shape[0], th)


def _post_mixer(x, a, b, k, v, wmix, gx, wq, qn, wo, gf, wi, wf, batch, tm, th):
    m, d = x.shape
    ka, kb = a.shape[1], b.shape[1]
    per = (m // batch) // tm
    nm = k.arr.shape[1] // batch
    hw = X_HEADS * X_DH
    row = lambda i: (i, 0)
    kv = pl.BlockSpec((None, nm, hw), lambda i: (k.idx, i // per, 0))
    return _call(
        functools.partial(_post_body, ka, th), (m // tm,),
        [(x, pl.BlockSpec((tm, d), row)), (a, pl.BlockSpec((tm, ka), row)),
         (b, pl.BlockSpec((tm, kb), row)), (k.arr, kv), (v.arr, kv)],
        [_whole(wmix, True), _whole(gx), _whole(wq, True), _whole(qn), _whole(wo, True), _whole(gf),
         _whole(wi, True), _whole(wf, True)],
        pl.BlockSpec((tm, d), row), jax.ShapeDtypeStruct((m, d), F32), ("parallel",), "post_mixer")


def _row(v):
    return v.reshape(1, -1).astype(F32)


def _block_diag(w):
    n, d, e = w.shape
    eye = jnp.eye(n, dtype=w.dtype)
    return (eye[:, None, :, None] * w[:, :, None, :]).reshape(n * d, n * e)


EV_QKV = 3 * A_HEADS * A_D
EV_Z = A_HEADS * A_D


def _even_w_in(w_in):
    o = EV_QKV + EV_Z
    w_bd = w_in[..., o:o + 2 * A_HEADS]
    w_bd = jnp.pad(w_bd, ((0, 0), (0, 0), (0, LANES - 2 * A_HEADS)))
    return jnp.concatenate([w_in[..., :o], w_in[..., o + 2 * A_HEADS:], w_bd], axis=-1).astype(BF16)


def _even_layer(x, batch, tm, tb, norm_g, w, conv_qkv, a_log, dt_bias, o_norm, conv_b_w, conv_b_b,
                gate_a_w, gate_a_b, gate_x_w, gate_x_b, lru_l):
    qkv, z, xb, gb, bd = _norm_proj(x, norm_g, w, [EV_QKV, EV_Z, B_WIDTH, B_WIDTH, LANES],
                                    [F32] * 5, tm)
    pad = jnp.zeros((A_HEADS,), F32)
    avec = _row(jnp.pad(jnp.concatenate([pad, a_log.astype(F32)]), (0, LANES - 2 * A_HEADS)))
    dvec = _row(jnp.pad(jnp.concatenate([pad, dt_bias.astype(F32)]), (0, LANES - 2 * A_HEADS)))
    y_a = _delta_rule(qkv, z, bd, conv_qkv.astype(F32), avec, dvec, _row(o_norm), batch, tb)
    y_b = _rglru(xb, gb, conv_b_w.astype(F32), _row(conv_b_b), _block_diag(gate_a_w).astype(BF16),
                 _row(gate_a_b), _block_diag(gate_x_w).astype(BF16), _row(gate_x_b), _row(lru_l),
                 batch, min(TB_LRU, xb.shape[0] // batch))
    return y_a, y_b


def _head_pad_cols(w, heads, real, total):
    lead = w.shape[:-1]
    w = w.reshape(lead + (heads, real))
    pad = [(0, 0)] * (w.ndim - 1) + [(0, total - real)]
    return jnp.pad(w, pad).reshape(lead + (heads * total,))


def _odd_weights(w_in, w_uq, w_ukv):
    hw = C_HEADS * 2 * C_DK
    cut = 3 * hw + Q_LORA + KV_LORA
    k_rope_w = jnp.pad(w_in[..., cut:], ((0, 0), (0, 0), (D_NOPE, LANES - D_QK)))
    w = jnp.concatenate([w_in[..., :cut], k_rope_w], axis=-1).astype(BF16)
    wuq = _head_pad_cols(w_uq, D_HEADS, D_QK, LANES).astype(BF16)
    kvr = w_ukv.reshape(w_ukv.shape[:-1] + (D_HEADS, D_NOPE + LANES))
    flat = lambda a: a.reshape(a.shape[:-2] + (-1,))
    wukv = jnp.concatenate([_head_pad_cols(flat(kvr[..., :D_NOPE]), D_HEADS, D_NOPE, LANES),
                            flat(kvr[..., D_NOPE:])], axis=-1).astype(BF16)
    return w, wuq, wukv


def _odd_layer(x, rope, pos_col, pos_row, batch, tm, tq, layer_idx, norm_g, w, c_q_norm, c_k_norm,
               lam_q1, lam_k1, lam_q2, lam_k2, c_sub_norm, q_lat_norm, wuq, kv_lat_norm, wukv,
               d_q_norm, d_k_norm):
    tile = lambda v, n: _row(jnp.tile(v.astype(F32), n))
    padn = lambda v: jnp.pad(v.astype(F32), (0, LANES - D_QK))
    qc, kc, vc, qd, kd, vd = _odd_in(
        x, rope[0], rope[1], norm_g, w, tile(c_q_norm, 2 * C_HEADS), tile(c_k_norm, 2 * C_HEADS),
        _row(q_lat_norm), wuq, _row(kv_lat_norm), wukv, tile(padn(d_q_norm), D_HEADS),
        tile(padn(d_k_norm), D_HEADS), tm, (min(TK_DIFF, tm), min(TK_MLA, tm)))
    lam_init = 0.8 - 0.6 * math.exp(-0.3 * layer_idx)
    lamv = jnp.stack([lam_q1, lam_k1, lam_q2, lam_k2]).astype(F32)
    nb_d = NB_MLA if batch % NB_MLA == 0 else 1
    yc = _flash(qc, kc, vc, batch, tq, nb_d, 2, (lamv, _row(c_sub_norm), pos_col, pos_row), lam_init)
    yd = _flash(qd, kd, vd, batch, tq, nb_d, 1)
    return yc, yd


def kernel(x, mem, positions, norm_mix, norm_x, norm_mem, x_wq, x_wkv, x_q_norm, x_k_norm, x_wo, norm_ffn, ffn_w_in, ffn_w_out, ev_w_in, ev_conv_qkv, ev_a_log, ev_dt_bias, ev_o_norm, ev_conv_b_w, ev_conv_b_b, ev_gate_a_w, ev_gate_a_b, ev_gate_x_w, ev_gate_x_b, ev_lru_l, ev_w_out, od_w_in, od_c_q_norm, od_c_k_norm, od_lam_q1, od_lam_k1, od_lam_q2, od_lam_k2, od_c_sub_norm, od_q_lat_norm, od_w_uq, od_kv_lat_norm, od_w_ukv, od_d_q_norm, od_d_k_norm, od_w_out):
    batch, seq, d = x.shape
    depth = norm_mix.shape[0]
    m = batch * seq
    tm = min(512, seq)
    tb = min(512, seq)
    tq = min(T_Q, seq)
    hid = ffn_w_out.shape[1]
    th = TH_FFN if hid % TH_FFN == 0 else hid
    tm_post = min(TM_POST, seq)
    xf = x.reshape(m, d).astype(F32)
    mem2 = mem.reshape(-1, d).astype(F32)
    pos_col = positions.reshape(m, 1).astype(jnp.int32)
    pos_row = positions.reshape(batch, seq // tq, 1, tq).astype(jnp.int32)
    rope = _rope_tables(pos_col, min(TM_ROPE, seq)) if depth > 1 else None
    vec = lambda a: a.astype(F32).reshape(a.shape[0], 1, -1)
    n_x, n_mem, n_ffn, n_mix = vec(norm_x), vec(norm_mem), vec(norm_ffn), vec(norm_mix)
    ev_w = _even_w_in(ev_w_in)
    od_w, od_wuq, od_wukv = _odd_weights(od_w_in, od_w_uq, od_w_ukv)
    qn_all = vec(jnp.tile(x_q_norm, (1, X_HEADS)))
    kn_all = vec(jnp.tile(x_k_norm, (1, X_HEADS)))
    wq_all, wkv_all, wo_all = x_wq.astype(BF16), x_wkv.astype(BF16), x_wo.astype(BF16)
    wi_all, wf_all = ffn_w_in.astype(BF16), ffn_w_out.astype(BF16)
    wmix_all = (ev_w_out.astype(BF16), od_w_out.astype(BF16))
    k_mem, v_mem = _mem_kv(mem2, n_mem[:depth], wkv_all[:depth], kn_all[:depth])
    for l in range(depth):
        i = l // 2
        if l % 2 == 0:
            y_a, y_b = _even_layer(xf, batch, tm, tb, _Layer(n_mix, l), _Layer(ev_w, i), ev_conv_qkv[i], ev_a_log[i],
                                   ev_dt_bias[i], ev_o_norm[i], ev_conv_b_w[i], ev_conv_b_b[i],
                                   ev_gate_a_w[i], ev_gate_a_b[i], ev_gate_x_w[i], ev_gate_x_b[i],
                                   ev_lru_l[i])
        else:
            y_a, y_b = _odd_layer(xf, rope, pos_col, pos_row, batch, min(TM_ODD, seq), tq, l, _Layer(n_mix, l), _Layer(od_w, i),
                                  od_c_q_norm[i], od_c_k_norm[i], od_lam_q1[i], od_lam_k1[i], od_lam_q2[i],
                                  od_lam_k2[i], od_c_sub_norm[i], od_q_lat_norm[i], _Layer(od_wuq, i),
                                  od_kv_lat_norm[i], _Layer(od_wukv, i), od_d_q_norm[i], od_d_k_norm[i])
        xf = _post_mixer(xf, y_a, y_b, _Layer(k_mem, l), _Layer(v_mem, l), _Layer(wmix_all[l % 2], i), _Layer(n_x, l),
                         _Layer(wq_all, l), _Layer(qn_all, l), _Layer(wo_all, l), _Layer(n_ffn, l),
                         _Layer(wi_all, l), _Layer(wf_all, l), batch, tm_post, th)
    return xf.reshape(batch, seq, d).astype(x.dtype)
```

```python
import functools
import math
from typing import NamedTuple

import jax
import jax.numpy as jnp
import numpy as np
from jax import lax
from jax.experimental import pallas as pl
from jax.experimental.pallas import tpu as pltpu

F32 = jnp.float32
BF16 = jnp.bfloat16

NORM_EPS = 1e-6
NEG_INF = -1e30
CHUNK = 64
LANES = 128
SUBLANES = 8
LRU_C = 8.0
ROPE_THETA = 10000.0
VMEM_LIMIT = 56 * 1024 * 1024

A_HEADS = 4
A_D = 128
B_WIDTH = 512
B_BLOCKS = 8
C_HEADS = 4
C_DK = 64
D_HEADS = 4
D_NOPE = 64
D_ROPE = 32
D_QK = D_NOPE + D_ROPE
Q_LORA = 256
KV_LORA = 128
X_HEADS = 4
X_DH = 128


def _params(*sem):
    return pltpu.CompilerParams(dimension_semantics=sem, vmem_limit_bytes=VMEM_LIMIT)


def _rms(x, g):
    ms = jnp.mean(x * x, axis=-1, keepdims=True)
    return x * lax.rsqrt(ms + NORM_EPS) * g


def _dot(a, b):
    return jnp.dot(a, b, preferred_element_type=F32)


def _dot_nt(a, b):
    return lax.dot_general(a, b, (((1,), (1,)), ((), ())), preferred_element_type=F32)


def _sigmoid(x):
    return 0.5 * jnp.tanh(0.5 * x) + 0.5


def _silu(x):
    h = 0.5 * x
    return h + h * jnp.tanh(h)


def _softplus(x):
    return jnp.maximum(x, 0.0) + jnp.log(1.0 + jnp.exp(-jnp.abs(x)))


def _full(shape):
    return pl.BlockSpec(shape, lambda *_: (0,) * len(shape))


class _Layer(NamedTuple):
    arr: jax.Array
    idx: int


def _whole(a, resident=False):
    kw = dict(pipeline_mode=pl.Buffered(1)) if resident else {}
    if isinstance(a, _Layer):
        tail = a.arr.shape[1:]
        return a.arr, pl.BlockSpec((None,) + tail, lambda *_: (a.idx,) + (0,) * len(tail), **kw)
    return a, pl.BlockSpec(a.shape, lambda *_: (0,) * a.ndim, **kw)


def _call(body, grid, tiled, whole, out_specs, out_shape, sem, name, scratch=()):
    ops = list(tiled) + list(whole)
    return pl.pallas_call(
        body, grid=grid, in_specs=[s for _, s in ops], out_specs=out_specs, out_shape=out_shape,
        scratch_shapes=list(scratch), compiler_params=_params(*sem), name=name,
    )(*[a for a, _ in ops])


def _norm_proj_body(splits, x_ref, g_ref, w_ref, *out_refs):
    h = _rms(x_ref[...], g_ref[...]).astype(BF16)
    for (start, width), o_ref in zip(splits, out_refs):
        o_ref[...] = _dot(h, w_ref[:, start:start + width]).astype(o_ref.dtype)


def _norm_proj(x, g, w, widths, dtypes, tm):
    m, d = x.shape
    starts = np.concatenate([[0], np.cumsum(widths)[:-1]]).tolist()
    splits = tuple(zip(starts, widths))
    return _call(
        functools.partial(_norm_proj_body, splits), (m // tm,),
        [(x, pl.BlockSpec((tm, d), lambda i: (i, 0)))], [_whole(g), _whole(w)],
        [pl.BlockSpec((tm, wd), lambda i: (i, 0)) for wd in widths],
        [jax.ShapeDtypeStruct((m, wd), dt) for wd, dt in zip(widths, dtypes)],
        ("parallel",), "norm_proj")


HALO = 8
TB_LRU = 1024


def _causal_conv(x_ref, w_ref, hist_ref, cols=slice(None)):
    x = x_ref[:, cols]
    tb = x.shape[0]
    k = w_ref.shape[0]
    full = jnp.concatenate([hist_ref[:, cols], x], axis=0)
    hist_ref[:, cols] = x[tb - HALO:, :]
    y = x * w_ref[k - 1:k, cols]
    for j in range(k - 1):
        y = y + pltpu.roll(full, k - 1 - j, 0)[HALO:, :] * w_ref[j:j + 1, cols]
    return y


def _l2_heads(x, nblk):
    outs = []
    for h in range(nblk):
        xh = x[:, h * LANES:(h + 1) * LANES]
        outs.append(xh * lax.rsqrt(jnp.sum(xh * xh, axis=-1, keepdims=True) + NORM_EPS))
    return jnp.concatenate(outs, axis=1)


def _delta_body(tb, qkv_ref, z_ref, bd_ref, cw_ref, avec_ref, dvec_ref, onorm_ref, tri_ref,
                y_ref, hist_ref, state_ref):
    @pl.when(pl.program_id(1) == 0)
    def _():
        state_ref[...] = jnp.zeros_like(state_ref)
        hist_ref[...] = jnp.zeros_like(hist_ref)

    nh, d = A_HEADS, A_D
    qk_w = nh * d

    def act(i):
        return _silu(_causal_conv(qkv_ref, cw_ref, hist_ref, slice(i * qk_w, (i + 1) * qk_w)))

    q_all = _l2_heads(act(0), nh) * d ** -0.5
    k_all = _l2_heads(act(1), nh)
    v_all = act(2)
    bd = bd_ref[...]
    beta_all = _sigmoid(bd)
    g_all = -jnp.exp(avec_ref[...]) * _softplus(bd + dvec_ref[...])
    tr = tri_ref.shape[0]
    cum_all = jnp.concatenate(
        [jnp.dot(tri_ref[...], g_all[r:r + tr], precision=lax.Precision.HIGHEST,
                 preferred_element_type=F32) for r in range(0, tb, tr)], axis=0)
    cum_t = cum_all.T

    ri = lax.broadcasted_iota(jnp.int32, (CHUNK, CHUNK), 0)
    ci = lax.broadcasted_iota(jnp.int32, (CHUNK, CHUNK), 1)
    incl = ri >= ci
    strict = ri > ci
    eye = jnp.where(ri == ci, 1.0, 0.0).astype(F32)

    nc = tb // CHUNK
    probs = [(h, c) for c in range(nc) for h in range(nh)]
    a_mat, rhs_wu, q_dec, k_dec_t, qk, neg_m, last = {}, {}, {}, {}, {}, {}, {}
    for pr in probs:
        h, c = pr
        rows = slice(c * CHUNK, (c + 1) * CHUNK)
        cols = slice(h * d, (h + 1) * d)
        q, k, v = q_all[rows, cols], k_all[rows, cols], v_all[rows, cols]
        beta = beta_all[rows, h:h + 1]
        cum = cum_all[rows, nh + h:nh + h + 1]
        cum_row = cum_t[nh + h:nh + h + 1, c * CHUNK:(c + 1) * CHUNK]
        cum_last = cum[CHUNK - 1:CHUNK, :]
        decay = jnp.where(incl, jnp.exp(jnp.where(incl, cum - cum_row, 0.0)), 0.0)
        e_cum = jnp.exp(cum)
        kb = k * beta
        a_mat[pr] = _dot_nt(jnp.concatenate([q, kb], axis=0).astype(BF16), k.astype(BF16))
        qk[pr] = jnp.where(incl, a_mat[pr][:CHUNK] * decay, 0.0).astype(BF16)
        neg_m[pr] = jnp.where(strict, -a_mat[pr][CHUNK:] * decay, 0.0)
        rhs_wu[pr] = jnp.concatenate([kb * e_cum, v * beta], axis=1).astype(BF16)
        q_dec[pr] = q * e_cum
        k_dec_t[pr] = (k * jnp.exp(cum_last - cum)).T.astype(BF16)
        last[pr] = jnp.exp(cum_last)
    pw = dict(neg_m)
    t_inv = {pr: eye + neg_m[pr] for pr in probs}
    for _ in range(CHUNK.bit_length() - 2):
        for pr in probs:
            p_bf = pw[pr].astype(BF16)
            pw[pr] = _dot(p_bf, p_bf)
        for pr in probs:
            t_inv[pr] = _dot(t_inv[pr].astype(BF16), (eye + pw[pr]).astype(BF16))
    wu = {pr: _dot(t_inv[pr].astype(BF16), rhs_wu[pr]) for pr in probs}

    state = [state_ref[h] for h in range(nh)]
    outs = [[] for _ in range(nh)]
    for c in range(nc):
        ws = [_dot(jnp.concatenate([wu[h, c][:, :d], q_dec[h, c]], axis=0).astype(BF16),
                   state[h].astype(BF16)) for h in range(nh)]
        v_new = [(wu[h, c][:, d:] - ws[h][:CHUNK]).astype(BF16) for h in range(nh)]
        for h in range(nh):
            outs[h].append(ws[h][CHUNK:] + _dot(qk[h, c], v_new[h]))
        state = [state[h] * last[h, c] + _dot(k_dec_t[h, c], v_new[h]) for h in range(nh)]
    for h in range(nh):
        state_ref[h] = state[h]
        o = jnp.concatenate(outs[h], axis=0)
        o = _rms(o, onorm_ref[...]) * _silu(z_ref[:, h * d:(h + 1) * d])
        y_ref[:, h * d:(h + 1) * d] = o.astype(y_ref.dtype)


def _delta_rule(qkv, z, bd, conv_w, avec, dvec, onorm, batch, tb):
    m, c3 = qkv.shape
    s = m // batch
    nsb = s // tb
    tr = min(tb, LANES)
    cid = np.arange(tr) // CHUNK
    tri = jnp.asarray(((cid[:, None] == cid[None, :]) &
                       (np.arange(tr)[:, None] >= np.arange(tr)[None, :])).astype(np.float32))
    vw = A_HEADS * A_D
    row = lambda b, j: (b * nsb + j, 0)
    return pl.pallas_call(
        functools.partial(_delta_body, tb),
        grid=(batch, nsb),
        in_specs=[pl.BlockSpec((tb, c3), row), pl.BlockSpec((tb, vw), row),
                  pl.BlockSpec((tb, LANES), row), _full(conv_w.shape), _full(avec.shape),
                  _full(dvec.shape), _full(onorm.shape), _full(tri.shape)],
        out_specs=pl.BlockSpec((tb, vw), row),
        out_shape=jax.ShapeDtypeStruct((m, vw), BF16),
        scratch_shapes=[pltpu.VMEM((HALO, c3), F32),
                        pltpu.VMEM((A_HEADS, A_D, A_D), F32)],
        compiler_params=_params("parallel", "arbitrary"),
        name="delta_rule",
    )(qkv, z, bd, conv_w, avec, dvec, onorm, tri)


def _gelu_tanh(x):
    return 0.5 * x * (1.0 + jnp.tanh(math.sqrt(2.0 / math.pi) * (x + 0.044715 * (x * x * x))))


def _lru_body(tb, xb_ref, gb_ref, cw_ref, cb_ref, wa_ref, ba_ref, wx_ref, bx_ref, l_ref,
              y_ref, hist_ref, h_ref):
    @pl.when(pl.program_id(1) == 0)
    def _():
        h_ref[...] = jnp.zeros_like(h_ref)
        hist_ref[...] = jnp.zeros_like(hist_ref)

    xc = _causal_conv(xb_ref, cw_ref, hist_ref) + cb_ref[...]
    xc_bf = xc.astype(BF16)
    r = _sigmoid(_dot(xc_bf, wa_ref[...]) + ba_ref[...])
    i = _sigmoid(_dot(xc_bf, wx_ref[...]) + bx_ref[...])
    log_a = (-LRU_C) * r * _softplus(-l_ref[...])
    a = jnp.exp(log_a)
    th = jnp.tanh(log_a)
    u = jnp.sqrt(-2.0 * th / (1.0 - th)) * (i * xc)
    wdt = a.shape[1]
    a3 = a.reshape(tb // SUBLANES, SUBLANES, wdt)
    u3 = u.reshape(tb // SUBLANES, SUBLANES, wdt)
    sub = lax.broadcasted_iota(jnp.int32, a3.shape, 1)
    shift = 1
    while shift < SUBLANES:
        keep = sub >= shift
        a_prev = jnp.where(keep, pltpu.roll(a3, shift, 1), 1.0)
        u_prev = jnp.where(keep, pltpu.roll(u3, shift, 1), 0.0)
        u3 = u3 + a3 * u_prev
        a3 = a3 * a_prev
        shift *= 2
    carry = h_ref[...]
    groups = []
    for r in range(tb // SUBLANES):
        h_r = u3[r] + a3[r] * carry
        carry = h_r[SUBLANES - 1:SUBLANES, :]
        groups.append(h_r)
    h_ref[...] = carry
    hs = jnp.concatenate(groups, axis=0)
    y_ref[...] = (_gelu_tanh(gb_ref[...]) * hs).astype(y_ref.dtype)


def _rglru(xb, gb, cw, cb, wa, ba, wx, bx, lru_l, batch, tb):
    m, wdt = xb.shape
    nsb = (m // batch) // tb
    row = lambda b, j: (b * nsb + j, 0)
    return pl.pallas_call(
        functools.partial(_lru_body, tb),
        grid=(batch, nsb),
        in_specs=[pl.BlockSpec((tb, wdt), row), pl.BlockSpec((tb, wdt), row),
                  _full(cw.shape), _full(cb.shape), _full(wa.shape), _full(ba.shape),
                  _full(wx.shape), _full(bx.shape), _full(lru_l.shape)],
        out_specs=pl.BlockSpec((tb, wdt), row),
        out_shape=jax.ShapeDtypeStruct((m, wdt), BF16),
        scratch_shapes=[pltpu.VMEM((HALO, wdt), F32), pltpu.VMEM((1, wdt), F32)],
        compiler_params=_params("parallel", "arbitrary"),
        name="rglru",
    )(xb, gb, cw, cb, wa, ba, wx, bx, lru_l)


def _head_norm(x, nblk, denom):
    outs = []
    for h in range(nblk):
        xh = x[:, h * LANES:(h + 1) * LANES]
        ss = jnp.sum(xh * xh, axis=-1, keepdims=True)
        outs.append(xh * lax.rsqrt(ss * (1.0 / denom) + NORM_EPS))
    return jnp.concatenate(outs, axis=1)


def _group_norm(x, ones_ref, denom):
    ss = _dot((x * x).astype(BF16), ones_ref[...])
    return x * lax.rsqrt(ss * (1.0 / denom) + NORM_EPS)


def _group_ones(width, group):
    gid = np.arange(width) // group
    return jnp.asarray(gid[:, None] == gid[None, :], dtype=BF16)


def _rope_table_body(pos_ref, freq_ref, cos_ref, sin_ref):
    lane = lax.broadcasted_iota(jnp.int32, (1, LANES), 1)
    ang = pos_ref[...].astype(F32) * freq_ref[...]
    is_rope = (lane >= D_NOPE) & (lane < D_QK)
    cos_ref[...] = jnp.where(is_rope, jnp.cos(ang), 1.0)
    sin_raw = jnp.sin(ang)
    sin_ref[...] = jnp.where(is_rope, jnp.where(lane < D_NOPE + D_ROPE // 2, -sin_raw, sin_raw), 0.0)


def _rope_tables(pos, tm):
    m = pos.shape[0]
    half = D_ROPE // 2
    inv_freq = ROPE_THETA ** (-jnp.arange(half, dtype=F32) / half)
    freq = jnp.concatenate([jnp.zeros((D_NOPE,), F32), inv_freq, inv_freq,
                            jnp.zeros((LANES - D_QK,), F32)]).reshape(1, LANES)
    row = lambda i: (i, 0)
    return _call(
        _rope_table_body, (m // tm,), [(pos, pl.BlockSpec((tm, 1), row))], [_whole(freq)],
        [pl.BlockSpec((tm, LANES), row)] * 2, [jax.ShapeDtypeStruct((m, LANES), F32)] * 2,
        ("parallel",), "rope_tables")


def _rope_blocks(x, nblk, cos, sin):
    lane = lax.broadcasted_iota(jnp.int32, (1, LANES), 1)
    first_half = lane < D_NOPE + D_ROPE // 2
    outs = []
    for h in range(nblk):
        xh = x[:, h * LANES:(h + 1) * LANES]
        partner = jnp.where(first_half, pltpu.roll(xh, LANES - D_ROPE // 2, 1),
                            pltpu.roll(xh, D_ROPE // 2, 1))
        outs.append(xh * cos + partner * sin)
    return jnp.concatenate(outs, axis=1)


def _store_vt(vt_ref, v, t):
    vt = v.T
    for j in range(v.shape[0] // t):
        vt_ref[j] = vt[:, j * t:(j + 1) * t].astype(vt_ref.dtype)


def _odd_in_body(tks, x_ref, cos_ref, sin_ref, g_ref, w_ref, cqn_ref, ckn_ref, qln_ref, wuq_ref, kvn_ref,
                 wukv_ref, dqn_ref, dkn_ref, ones_c_ref, ones_d_ref,
                 qc_ref, kc_ref, vc_ref, qd_ref, kd_ref, vd_ref):
    hw = C_HEADS * LANES
    h = _rms(x_ref[...], g_ref[...]).astype(BF16)
    qc = _dot(h, w_ref[:, 0:hw])
    qc_ref[...] = (_group_norm(qc, ones_c_ref, C_DK) * (cqn_ref[...] * (C_DK ** -0.5 * LOG2E))).astype(BF16)
    kc = _dot(h, w_ref[:, hw:2 * hw])
    kc_ref[...] = (_group_norm(kc, ones_c_ref, C_DK) * ckn_ref[...]).astype(BF16)
    _store_vt(vc_ref, _dot(h, w_ref[:, 2 * hw:3 * hw]), tks[0])

    cos = cos_ref[...]
    sin = sin_ref[...]
    off = 3 * hw
    q_lat = _dot(h, w_ref[:, off:off + Q_LORA])
    qd = _dot(_rms(q_lat, qln_ref[...]).astype(BF16), wuq_ref[...])
    qd = _group_norm(qd, ones_d_ref, D_QK) * (dqn_ref[...] * (D_QK ** -0.5 * LOG2E))
    qd_ref[...] = _rope_blocks(qd, D_HEADS, cos, sin).astype(BF16)

    off += Q_LORA
    kv_lat = _dot(h, w_ref[:, off:off + KV_LORA])
    k_rope = _dot(h, w_ref[:, off + KV_LORA:off + KV_LORA + LANES])
    kvd = _dot(_rms(kv_lat, kvn_ref[...]).astype(BF16), wukv_ref[...])
    dhw = D_HEADS * LANES
    kd = kvd[:, :dhw] + jnp.concatenate([k_rope] * D_HEADS, axis=1)
    kd = _group_norm(kd, ones_d_ref, D_QK) * dkn_ref[...]
    kd_ref[...] = _rope_blocks(kd, D_HEADS, cos, sin).astype(BF16)
    _store_vt(vd_ref, kvd[:, dhw:], tks[1])


def _odd_in(x, cos, sin, g, w, cqn, ckn, qln, wuq, kvn, wukv, dqn, dkn, tm, tks):
    m, d = x.shape
    hw = C_HEADS * LANES
    row = lambda i: (i, 0)
    small = [g, w, cqn, ckn, qln, wuq, kvn, wukv, dqn, dkn, _group_ones(hw, C_DK), _group_ones(hw, LANES)]
    rspec = pl.BlockSpec((tm, hw), row)
    tspec = [pl.BlockSpec((tm // t, hw, t), lambda i: (i, 0, 0)) for t in tks]
    rshape = jax.ShapeDtypeStruct((m, hw), BF16)
    tshape = [jax.ShapeDtypeStruct((m // t, hw, t), BF16) for t in tks]
    tab = pl.BlockSpec((tm, LANES), row)
    return _call(
        functools.partial(_odd_in_body, tks), (m // tm,),
        [(x, pl.BlockSpec((tm, d), row)), (cos, tab), (sin, tab)],
        [_whole(a) for a in small],
        [rspec, rspec, tspec[0], rspec, rspec, tspec[1]],
        [rshape, rshape, tshape[0], rshape, rshape, tshape[1]],
        ("parallel",), "odd_in")


LOG2E = math.log2(math.e)
T_Q = 256
TK_DIFF = 256
TK_MLA = 512
SUM_ROWS = 16
QK_AHEAD = 6
NB_MLA = 2
TM_IN = 1024
TM_ROPE = 2048
TM_POST = 1024
TH_FFN = 256


def _flash_body(n_maps, alibi, t, nb, lam_init, *refs):
    if alibi:
        lam_ref, sub_ref, q_ref, k_ref, vt_ref, pq_ref, pk_ref, o_ref = refs
    else:
        q_ref, k_ref, vt_ref, o_ref = refs
    seq = q_ref.shape[0] // nb
    nq = seq // t
    nh = q_ref.shape[1] // LANES
    tk = vt_ref.shape[2]
    ng = seq // tk
    lane = lax.broadcasted_iota(jnp.int32, (1, LANES), 1)
    key_i = lax.broadcasted_iota(jnp.int32, (tk, t), 0)
    qry_i = lax.broadcasted_iota(jnp.int32, (tk, t), 1)
    shift = CHUNK.bit_length() - 1
    chains = [(b, h, mp) for b in range(nb) for h in range(nh) for mp in range(n_maps)]
    ones_rows = jnp.ones((SUM_ROWS, tk), BF16)
    hcols = lambda h: slice(h * LANES, (h + 1) * LANES)

    def q_block(qi, _):
        qm = {}
        for b in range(nb):
            rows = pl.ds(pl.multiple_of(b * seq + qi * t, t), t)
            for h in range(nh):
                q = q_ref[rows, hcols(h)]
                for mp in range(n_maps):
                    qm[b, h, mp] = (jnp.where((lane < C_DK) == (mp == 0), q, jnp.zeros_like(q))
                                    if n_maps == 2 else q)

        def group(g, carry, masked):
            krows = [pl.ds(pl.multiple_of(b * seq + g * tk, tk), tk) for b in range(nb)]
            if masked:
                allowed = ((key_i + g * tk) >> shift) <= ((qry_i + qi * t) >> shift)
            bias = {}

            def scores(ch):
                b, h, _ = ch
                s = _dot_nt(k_ref[krows[b], hcols(h)], qm[ch])
                if alibi:
                    if (b, h) not in bias:
                        dist = jnp.abs(pk_ref[krows[b], :].astype(F32) - pq_ref[b, qi].astype(F32))
                        bias[b, h] = (2.0 ** (-2 * (h + 1)) * LOG2E) * dist
                    s = s - bias[b, h]
                return jnp.where(allowed, s, NEG_INF) if masked else s

            s = {ch: scores(ch) for ch in chains[:QK_AHEAD]}
            new = []
            for i, (ch, (m_i, acc)) in enumerate(zip(chains, carry)):
                if i + QK_AHEAD < len(chains):
                    nxt = chains[i + QK_AHEAD]
                    s[nxt] = scores(nxt)
                b, h, _ = ch
                vt_ext = jnp.concatenate([vt_ref[b * ng + g, hcols(h), :], ones_rows], axis=0)
                s_ch = s.pop(ch)
                m_new = jnp.maximum(m_i, jnp.max(s_ch, axis=0, keepdims=True))
                alpha = jnp.exp2(m_i - m_new)
                p = jnp.exp2((s_ch - m_new).astype(BF16))
                new.append((m_new, alpha * acc + _dot(vt_ext, p)))
            return tuple(new)

        init = tuple((jnp.full((1, t), NEG_INF, F32), jnp.zeros((LANES + SUM_ROWS, t), F32))
                     for _ in chains)
        g_last = (qi * t) // tk
        carry = lax.fori_loop(0, g_last, lambda g, c: group(g, c, False), init)
        carry = group(g_last, carry, True)
        outs = {ch: acc[:LANES] / acc[LANES:LANES + 1] for ch, (_, acc) in zip(chains, carry)}
        if n_maps == 2:
            lv = lam_ref[...]
            lam = (jnp.exp(jnp.sum(lv[0:1] * lv[1:2], axis=-1, keepdims=True))
                   - jnp.exp(jnp.sum(lv[2:3] * lv[3:4], axis=-1, keepdims=True)) + lam_init)
        for b in range(nb):
            rows = pl.ds(pl.multiple_of(b * seq + qi * t, t), t)
            for h in range(nh):
                if n_maps == 2:
                    o = (outs[b, h, 0] - lam * outs[b, h, 1]).T
                    o = _rms(o, sub_ref[...]) * (1.0 - lam_init)
                else:
                    o = outs[b, h, 0].T
                o_ref[rows, hcols(h)] = o.astype(o_ref.dtype)
        return 0

    lax.fori_loop(0, nq, q_block, 0)


def _flash(q, k, vt, batch, t, nb, n_maps, extras=None, lam_init=0.0):
    m, hw = q.shape
    s = m // batch
    nq = s // t
    tk = vt.shape[2]
    alibi = extras is not None
    spec = pl.BlockSpec((nb * s, hw), lambda b: (b, 0))
    in_specs = [spec, spec, pl.BlockSpec((nb * s // tk, hw, tk), lambda b: (b, 0, 0))]
    args = [q, k, vt]
    if alibi:
        lamv, subn, pos_col, pos_row = extras
        in_specs = [_full(lamv.shape), _full(subn.shape)] + in_specs + [
            pl.BlockSpec((nb, nq, 1, t), lambda b: (b, 0, 0, 0)),
            pl.BlockSpec((nb * s, 1), lambda b: (b, 0))]
        args = [lamv, subn] + args + [pos_row, pos_col]
    return pl.pallas_call(
        functools.partial(_flash_body, n_maps, alibi, t, nb, lam_init),
        grid=(batch // nb,),
        in_specs=in_specs,
        out_specs=spec,
        out_shape=jax.ShapeDtypeStruct((m, hw), BF16),
        compiler_params=_params("parallel"),
        name="flash_diff" if alibi else "flash_mla",
    )(*args)


def _mem_kv_body(mem_ref, g_ref, w_ref, kn_ref, k_ref, v_ref):
    hw = X_HEADS * X_DH
    mn = _rms(mem_ref[...], g_ref[...]).astype(BF16)
    k = _dot(mn, w_ref[:, :hw])
    k_ref[...] = (_head_norm(k, X_HEADS, X_DH) * kn_ref[...]).astype(BF16)
    v_ref[...] = _dot(mn, w_ref[:, hw:]).astype(BF16)


def _mem_kv(mem2, g, w, kn):
    rows, d = mem2.shape
    layers = w.shape[0]
    hw = X_HEADS * X_DH
    per_layer = lambda a: (a, pl.BlockSpec((None,) + a.shape[1:], lambda l: (l, 0, 0)))
    return _call(
        _mem_kv_body, (layers,),
        [(mem2, pl.BlockSpec((rows, d), lambda l: (0, 0))), per_layer(g), per_layer(w), per_layer(kn)],
        [],
        [pl.BlockSpec((None, rows, hw), lambda l: (l, 0, 0))] * 2,
        [jax.ShapeDtypeStruct((layers, rows, hw), BF16)] * 2,
        ("parallel",), "mem_kv")


def _cross_update(x, g_ref, wq_ref, qn_ref, k_ref, v_ref, wo_ref):
    q = _dot(_rms(x, g_ref[...]).astype(BF16), wq_ref[...])
    q = (_head_norm(q, X_HEADS, X_DH) * (qn_ref[...] * X_DH ** -0.5)).astype(BF16)
    outs = []
    for h in range(X_HEADS):
        blk = slice(h * X_DH, (h + 1) * X_DH)
        s = _dot_nt(q[:, blk], k_ref[:, blk])
        p = jnp.exp(s - jnp.max(s, axis=-1, keepdims=True))
        l = jnp.sum(p, axis=-1, keepdims=True)
        outs.append((_dot(p.astype(BF16), v_ref[:, blk]) / l).astype(BF16))
    return x + _dot(jnp.concatenate(outs, axis=1), wo_ref[...])


def _ffn_update(x, g_ref, wi_ref, wo_ref, hid, th):
    h = _rms(x, g_ref[...]).astype(BF16)
    acc = x
    for c in range(hid // th):
        gate = _dot(h, wi_ref[:, c * th:(c + 1) * th])
        up = _dot(h, wi_ref[:, hid + c * th:hid + (c + 1) * th])
        acc = acc + _dot((_silu(gate) * up).astype(BF16), wo_ref[c * th:(c + 1) * th, :])
    return acc


def _post_body(ka, th, x_ref, a_ref, b_ref, k_ref, v_ref, wmix_ref, gx_ref, wq_ref, qn_ref,
               wo_ref, gf_ref, wi_ref, wf_ref, o_ref):
    a, b = a_ref[...], b_ref[...]
    mix = [_dot(a, wmix_ref[:ka, c:c + TH_FFN]) + _dot(b, wmix_ref[ka:, c:c + TH_FFN])
           for c in range(0, wmix_ref.shape[1], TH_FFN)]
    x = x_ref[...] + jnp.concatenate(mix, axis=1)
    x = _cross_update(x, gx_ref, wq_ref, qn_ref, k_ref, v_ref, wo_ref)
    o_ref[...] = _ffn_update(x, gf_ref, wi_ref, wf_ref, wf_ref.shape[0], th)


def _post_mixer(x, a, b, k, v, wmix, gx, wq, qn, wo, gf, wi, wf, batch, tm, th):
    m, d = x.shape
    ka, kb = a.shape[1], b.shape[1]
    per = (m // batch) // tm
    nm = k.arr.shape[1] // batch
    hw = X_HEADS * X_DH
    row = lambda i: (i, 0)
    kv = pl.BlockSpec((None, nm, hw), lambda i: (k.idx, i // per, 0))
    return _call(
        functools.partial(_post_body, ka, th), (m // tm,),
        [(x, pl.BlockSpec((tm, d), row)), (a, pl.BlockSpec((tm, ka), row)),
         (b, pl.BlockSpec((tm, kb), row)), (k.arr, kv), (v.arr, kv)],
        [_whole(wmix, True), _whole(gx), _whole(wq, True), _whole(qn), _whole(wo, True), _whole(gf),
         _whole(wi, True), _whole(wf, True)],
        pl.BlockSpec((tm, d), row), jax.ShapeDtypeStruct((m, d), F32), ("parallel",), "post_mixer")


def _row(v):
    return v.reshape(1, -1).astype(F32)


def _block_diag(w):
    n, d, e = w.shape
    eye = jnp.eye(n, dtype=w.dtype)
    return (eye[:, None, :, None] * w[:, :, None, :]).reshape(n * d, n * e)


EV_QKV = 3 * A_HEADS * A_D
EV_Z = A_HEADS * A_D


def _even_w_in(w_in):
    o = EV_QKV + EV_Z
    w_bd = w_in[..., o:o + 2 * A_HEADS]
    w_bd = jnp.pad(w_bd, ((0, 0), (0, 0), (0, LANES - 2 * A_HEADS)))
    return jnp.concatenate([w_in[..., :o], w_in[..., o + 2 * A_HEADS:], w_bd], axis=-1).astype(BF16)


def _even_layer(x, batch, tm, tb, norm_g, w, conv_qkv, a_log, dt_bias, o_norm, conv_b_w, conv_b_b,
                gate_a_w, gate_a_b, gate_x_w, gate_x_b, lru_l):
    qkv, z, xb, gb, bd = _norm_proj(x, norm_g, w, [EV_QKV, EV_Z, B_WIDTH, B_WIDTH, LANES],
                                    [F32] * 5, tm)
    pad = jnp.zeros((A_HEADS,), F32)
    avec = _row(jnp.pad(jnp.concatenate([pad, a_log.astype(F32)]), (0, LANES - 2 * A_HEADS)))
    dvec = _row(jnp.pad(jnp.concatenate([pad, dt_bias.astype(F32)]), (0, LANES - 2 * A_HEADS)))
    y_a = _delta_rule(qkv, z, bd, conv_qkv.astype(F32), avec, dvec, _row(o_norm), batch, tb)
    y_b = _rglru(xb, gb, conv_b_w.astype(F32), _row(conv_b_b), _block_diag(gate_a_w).astype(BF16),
                 _row(gate_a_b), _block_diag(gate_x_w).astype(BF16), _row(gate_x_b), _row(lru_l),
                 batch, min(TB_LRU, xb.shape[0] // batch))
    return y_a, y_b


def _head_pad_cols(w, heads, real, total):
    lead = w.shape[:-1]
    w = w.reshape(lead + (heads, real))
    pad = [(0, 0)] * (w.ndim - 1) + [(0, total - real)]
    return jnp.pad(w, pad).reshape(lead + (heads * total,))


def _odd_weights(w_in, w_uq, w_ukv):
    hw = C_HEADS * 2 * C_DK
    cut = 3 * hw + Q_LORA + KV_LORA
    k_rope_w = jnp.pad(w_in[..., cut:], ((0, 0), (0, 0), (D_NOPE, LANES - D_QK)))
    w = jnp.concatenate([w_in[..., :cut], k_rope_w], axis=-1).astype(BF16)
    wuq = _head_pad_cols(w_uq, D_HEADS, D_QK, LANES).astype(BF16)
    kvr = w_ukv.reshape(w_ukv.shape[:-1] + (D_HEADS, D_NOPE + LANES))
    flat = lambda a: a.reshape(a.shape[:-2] + (-1,))
    wukv = jnp.concatenate([_head_pad_cols(flat(kvr[..., :D_NOPE]), D_HEADS, D_NOPE, LANES),
                            flat(kvr[..., D_NOPE:])], axis=-1).astype(BF16)
    return w, wuq, wukv


def _odd_layer(x, rope, pos_col, pos_row, batch, tm, tq, layer_idx, norm_g, w, c_q_norm, c_k_norm,
               lam_q1, lam_k1, lam_q2, lam_k2, c_sub_norm, q_lat_norm, wuq, kv_lat_norm, wukv,
               d_q_norm, d_k_norm):
    tile = lambda v, n: _row(jnp.tile(v.astype(F32), n))
    padn = lambda v: jnp.pad(v.astype(F32), (0, LANES - D_QK))
    qc, kc, vc, qd, kd, vd = _odd_in(
        x, rope[0], rope[1], norm_g, w, tile(c_q_norm, 2 * C_HEADS), tile(c_k_norm, 2 * C_HEADS),
        _row(q_lat_norm), wuq, _row(kv_lat_norm), wukv, tile(padn(d_q_norm), D_HEADS),
        tile(padn(d_k_norm), D_HEADS), tm, (min(TK_DIFF, tm), min(TK_MLA, tm)))
    lam_init = 0.8 - 0.6 * math.exp(-0.3 * layer_idx)
    lamv = jnp.stack([lam_q1, lam_k1, lam_q2, lam_k2]).astype(F32)
    nb_d = NB_MLA if batch % NB_MLA == 0 else 1
    yc = _flash(qc, kc, vc, batch, tq, nb_d, 2, (lamv, _row(c_sub_norm), pos_col, pos_row), lam_init)
    yd = _flash(qd, kd, vd, batch, tq, nb_d, 1)
    return yc, yd


def kernel(x, mem, positions, norm_mix, norm_x, norm_mem, x_wq, x_wkv, x_q_norm, x_k_norm, x_wo, norm_ffn, ffn_w_in, ffn_w_out, ev_w_in, ev_conv_qkv, ev_a_log, ev_dt_bias, ev_o_norm, ev_conv_b_w, ev_conv_b_b, ev_gate_a_w, ev_gate_a_b, ev_gate_x_w, ev_gate_x_b, ev_lru_l, ev_w_out, od_w_in, od_c_q_norm, od_c_k_norm, od_lam_q1, od_lam_k1, od_lam_q2, od_lam_k2, od_c_sub_norm, od_q_lat_norm, od_w_uq, od_kv_lat_norm, od_w_ukv, od_d_q_norm, od_d_k_norm, od_w_out):
    batch, seq, d = x.shape
    depth = norm_mix.shape[0]
    m = batch * seq
    tm = min(512, seq)
    tb = min(512, seq)
    tq = min(T_Q, seq)
    hid = ffn_w_out.shape[1]
    th = TH_FFN if hid % TH_FFN == 0 else hid
    tm_post = min(TM_POST, seq)
    xf = x.reshape(m, d).astype(F32)
    mem2 = mem.reshape(-1, d).astype(F32)
    pos_col = positions.reshape(m, 1).astype(jnp.int32)
    pos_row = positions.reshape(batch, seq // tq, 1, tq).astype(jnp.int32)
    rope = _rope_tables(pos_col, min(TM_ROPE, seq)) if depth > 1 else None
    vec = lambda a: a.astype(F32).reshape(a.shape[0], 1, -1)
    n_x, n_mem, n_ffn, n_mix = vec(norm_x), vec(norm_mem), vec(norm_ffn), vec(norm_mix)
    ev_w = _even_w_in(ev_w_in)
    od_w, od_wuq, od_wukv = _odd_weights(od_w_in, od_w_uq, od_w_ukv)
    qn_all = vec(jnp.tile(x_q_norm, (1, X_HEADS)))
    kn_all = vec(jnp.tile(x_k_norm, (1, X_HEADS)))
    wq_all, wkv_all, wo_all = x_wq.astype(BF16), x_wkv.astype(BF16), x_wo.astype(BF16)
    wi_all, wf_all = ffn_w_in.astype(BF16), ffn_w_out.astype(BF16)
    wmix_all = (ev_w_out.astype(BF16), od_w_out.astype(BF16))
    k_mem, v_mem = _mem_kv(mem2, n_mem[:depth], wkv_all[:depth], kn_all[:depth])
    for l in range(depth):
        i = l // 2
        if l % 2 == 0:
            y_a, y_b = _even_layer(xf, batch, min(TM_IN, seq), tb, _Layer(n_mix, l), _Layer(ev_w, i), ev_conv_qkv[i], ev_a_log[i],
                                   ev_dt_bias[i], ev_o_norm[i], ev_conv_b_w[i], ev_conv_b_b[i],
                                   ev_gate_a_w[i], ev_gate_a_b[i], ev_gate_x_w[i], ev_gate_x_b[i],
                                   ev_lru_l[i])
        else:
            y_a, y_b = _odd_layer(xf, rope, pos_col, pos_row, batch, min(TM_IN, seq), tq, l, _Layer(n_mix, l), _Layer(od_w, i),
                                  od_c_q_norm[i], od_c_k_norm[i], od_lam_q1[i], od_lam_k1[i], od_lam_q2[i],
                                  od_lam_k2[i], od_c_sub_norm[i], od_q_lat_norm[i], _Layer(od_wuq, i),
                                  od_kv_lat_norm[i], _Layer(od_wukv, i), od_d_q_norm[i], od_d_k_norm[i])
        xf = _post_mixer(xf, y_a, y_b, _Layer(k_mem, l), _Layer(v_mem, l), _Layer(wmix_all[l % 2], i), _Layer(n_x, l),
                         _Layer(wq_all, l), _Layer(qn_all, l), _Layer(wo_all, l), _Layer(n_ffn, l),
                         _Layer(wi_all, l), _Layer(wf_all, l), batch, tm_post, th)
    return xf.reshape(batch, seq, d).astype(x.dtype)
```
